```python
import math
import jax
import jax.numpy as jnp
from jax import lax
import numpy as np

D_MODEL = 1024
BATCH = 8
SEQ = 4096
DEPTH = 2

S5_WIDTH = D_MODEL // 2
S5_GROUP_SIZE = 16
S5_GROUPS = S5_WIDTH // S5_GROUP_SIZE
S5_STATE = 64
S5_DT_MIN = 1e-3
S5_DT_MAX = 1e-1
POOL_WIDTH = D_MODEL // 2
POOL_WINDOWS = (2, 4, 8, 16)
POOL_GROUP_SIZE = POOL_WIDTH // len(POOL_WINDOWS)
ATTN_HEADS = 16
HEAD_DIM = 64
ATTN_WIDTH = ATTN_HEADS * HEAD_DIM
ATTN_PATTERNS = ((128, 1), (512, 4), (2048, 16))
ATTN_BLOCK = 64
N_BRANCHES = 3
IN_WIDTH = S5_WIDTH + POOL_WIDTH + 3 * ATTN_WIDTH + N_BRANCHES * D_MODEL
D_FF = 4 * D_MODEL
NORM_EPS = 1e-6
NEG_INF = -1e30

kernel_name = 'hybrid_s5_pool_dilated_attn_encoder'


def rmsnorm(x, g):
    xf = x.astype(jnp.float32)
    y = xf * lax.rsqrt(jnp.mean(xf * xf, axis=-1, keepdims=True) + NORM_EPS)
    return (y * g.astype(jnp.float32)).astype(x.dtype)


def _diag_recurrence(e1, e2):
    a1, b1 = e1
    a2, b2 = e2
    return a1 * a2, a2 * b1 + b2


def s5_direction(ug, lam_re, lam_im, log_dt, b_re, b_im, c_re, c_im, reverse):
    f32 = jnp.float32
    L = ug.shape[1]
    lam = lax.complex(lam_re.astype(f32), lam_im.astype(f32))
    dt = jnp.exp(log_dt.astype(f32))[:, None]
    a_bar = jnp.exp(lam * dt)
    b = lax.complex(b_re.astype(f32), b_im.astype(f32))
    b_bar = ((a_bar - 1.0) / lam)[:, :, None] * b
    bu = jnp.einsum('blgp,gnp->blgn', ug.astype(jnp.complex64), b_bar)
    if reverse:
        bu = jnp.flip(bu, axis=1)
    a_seq = jnp.broadcast_to(a_bar[None, None], (1, L) + a_bar.shape)
    _, states = lax.associative_scan(_diag_recurrence, (a_seq, bu), axis=1)
    if reverse:
        states = jnp.flip(states, axis=1)
    c = lax.complex(c_re.astype(f32), c_im.astype(f32))
    return jnp.einsum('blgn,gpn->blgp', states, c).real


def s5_mixer(u, lam_re, lam_im, log_dt, b_re, b_im, c_re, c_im, d_skip, w_glu):
    bsz, L, _ = u.shape
    uf = u.astype(jnp.float32)
    ug = uf.reshape(bsz, L, S5_GROUPS, S5_GROUP_SIZE)
    y_fwd = s5_direction(ug, lam_re[0], lam_im[0], log_dt[0], b_re[0], b_im[0], c_re[0], c_im[0], False)
    y_bwd = s5_direction(ug, lam_re[1], lam_im[1], log_dt[1], b_re[1], b_im[1], c_re[1], c_im[1], True)
    y = (y_fwd + y_bwd).reshape(bsz, L, S5_WIDTH) + d_skip.astype(jnp.float32) * uf
    y = jax.nn.gelu(y)
    y = y * jax.nn.sigmoid(y @ w_glu.astype(jnp.float32))
    return y.astype(u.dtype)


def pool_mixer(u, w_group, scale):
    bsz, L, _ = u.shape
    ng = len(POOL_WINDOWS)
    uf = u.astype(jnp.float32).reshape(bsz, L, ng, POOL_GROUP_SIZE)
    cs = jnp.concatenate([jnp.zeros((bsz, 1, ng, POOL_GROUP_SIZE), jnp.float32),
                          jnp.cumsum(uf, axis=1)], axis=1)
    t = jnp.arange(L)
    outs = []
    for g, win in enumerate(POOL_WINDOWS):
        left = win // 2
        right = win - 1 - left
        hi = jnp.minimum(t + right + 1, L)
        lo = jnp.maximum(t - left, 0)
        csg = cs[:, :, g]
        window_sum = jnp.take(csg, hi, axis=1) - jnp.take(csg, lo, axis=1)
        mean = window_sum / (hi - lo).astype(jnp.float32)[None, :, None]
        outs.append(mean - uf[:, :, g])
    pooled = jnp.stack(outs, axis=2)
    mixed = jnp.einsum('blgc,gcd->blgd', pooled, w_group.astype(jnp.float32))
    return (mixed.reshape(bsz, L, POOL_WIDTH) * scale.astype(jnp.float32)).astype(u.dtype)


def alibi_slopes(n_heads):
    return jnp.exp2(-8.0 * jnp.arange(1, n_heads + 1, dtype=jnp.float32) / n_heads)


def dilated_band(q, k, v, window, dil, slopes):
    bsz, L, H, E = q.shape
    radius = (window // 2) // dil
    QB = ATTN_BLOCK
    M = L // dil
    nb = -(-M // QB)
    Mp = nb * QB

    def by_residue(t):
        return t.reshape(bsz, M, dil, H, E).transpose(0, 2, 1, 3, 4)

    pad5 = lambda lo, hi: ((0, 0), (0, 0), (lo, hi), (0, 0), (0, 0))
    qb = jnp.pad(by_residue(q), pad5(0, Mp - M)).reshape(bsz, dil, nb, QB, H, E)
    kp = jnp.pad(by_residue(k), pad5(QB, Mp - M + QB)).reshape(bsz, dil, nb + 2, QB, H, E)
    vp = jnp.pad(by_residue(v), pad5(QB, Mp - M + QB)).reshape(bsz, dil, nb + 2, QB, H, E)
    kb = jnp.concatenate([kp[:, :, :-2], kp[:, :, 1:-1], kp[:, :, 2:]], axis=3)
    vb = jnp.concatenate([vp[:, :, :-2], vp[:, :, 1:-1], vp[:, :, 2:]], axis=3)

    a = jnp.arange(QB)
    c = jnp.arange(3 * QB)
    rel = c[None, :] - QB - a[:, None]
    key_m = jnp.arange(nb)[:, None] * QB + c[None, :] - QB
    valid = (jnp.abs(rel) <= radius)[None] & ((key_m >= 0) & (key_m < M))[:, None, :]
    dist = (dil * jnp.abs(rel)).astype(jnp.float32)

    scores = jnp.einsum('bdiqhe,bdikhe->bdihqk', qb, kb)
    scores = scores - slopes[:, None, None] * dist[None]
    scores = jnp.where(valid[None, None, :, None], scores, NEG_INF)
    mx = jnp.max(scores, axis=-1)
    p = jnp.exp(scores - mx[..., None])
    den = jnp.sum(p, axis=-1)
    num = jnp.einsum('bdihqk,bdikhe->bdiqhe', p, vb)

    num = num.reshape(bsz, dil, Mp, H, E)[:, :, :M].transpose(0, 2, 1, 3, 4).reshape(bsz, L, H, E)

    def back(s):
        s = s.transpose(0, 1, 2, 4, 3).reshape(bsz, dil, Mp, H)[:, :, :M]
        return s.transpose(0, 2, 1, 3).reshape(bsz, L, H)

    return num, back(mx), back(den)


def dilated_attention(qkv):
    bsz, L, _ = qkv.shape
    q, k, v = jnp.split(qkv.astype(jnp.float32), 3, axis=-1)
    q = q.reshape(bsz, L, ATTN_HEADS, HEAD_DIM) * (HEAD_DIM ** -0.5)
    k = k.reshape(bsz, L, ATTN_HEADS, HEAD_DIM)
    v = v.reshape(bsz, L, ATTN_HEADS, HEAD_DIM)
    slopes = alibi_slopes(ATTN_HEADS)
    nums, mxs, dens = [], [], []
    for window, dil in ATTN_PATTERNS:
        num, mx, den = dilated_band(q, k, v, window, dil, slopes)
        nums.append(num)
        mxs.append(mx)
        dens.append(den)
    mxs = jnp.stack(mxs)
    w = jnp.exp(mxs - jnp.max(mxs, axis=0, keepdims=True))
    total_num = jnp.sum(w[..., None] * jnp.stack(nums), axis=0)
    total_den = jnp.sum(w * jnp.stack(dens), axis=0)
    out = total_num / total_den[..., None]
    return out.reshape(bsz, L, ATTN_WIDTH).astype(qkv.dtype)


def setup_inputs(seed: int = 0) -> dict:
    key = jax.random.key(seed)
    ks = jax.random.split(key, 24)
    f32 = jnp.float32

    def nrm(k, shape, scale):
        return jax.random.normal(k, shape, f32) * scale

    G, N, P = S5_GROUPS, S5_STATE, S5_GROUP_SIZE
    n_idx = jnp.arange(N, dtype=f32)
    return {
        'x': nrm(ks[0], (BATCH, SEQ, D_MODEL), 1.0),
        'norm_mix': 1.0 + nrm(ks[1], (DEPTH, D_MODEL), 0.02),
        'w_in': nrm(ks[2], (DEPTH, D_MODEL, IN_WIDTH), D_MODEL ** -0.5),
        's5_lam_re': -0.5 + nrm(ks[3], (DEPTH, 2, G, N), 0.01),
        's5_lam_im': math.pi * n_idx + nrm(ks[4], (DEPTH, 2, G, N), 0.01),
        's5_log_dt': jax.random.uniform(ks[5], (DEPTH, 2, G), f32, math.log(S5_DT_MIN), math.log(S5_DT_MAX)),
        's5_b_re': nrm(ks[6], (DEPTH, 2, G, N, P), (2 * P) ** -0.5),
        's5_b_im': nrm(ks[7], (DEPTH, 2, G, N, P), (2 * P) ** -0.5),
        's5_c_re': nrm(ks[8], (DEPTH, 2, G, P, N), N ** -0.5),
        's5_c_im': nrm(ks[9], (DEPTH, 2, G, P, N), N ** -0.5),
        's5_d': nrm(ks[10], (DEPTH, S5_WIDTH), 1.0),
        's5_w_glu': nrm(ks[11], (DEPTH, S5_WIDTH, S5_WIDTH), S5_WIDTH ** -0.5),
        'pool_w': nrm(ks[12], (DEPTH, len(POOL_WINDOWS), POOL_GROUP_SIZE, POOL_GROUP_SIZE), POOL_GROUP_SIZE ** -0.5),
        'pool_scale': 1.0 + nrm(ks[13], (DEPTH, POOL_WIDTH), 0.1),
        'w_branch_s5': nrm(ks[14], (DEPTH, S5_WIDTH, D_MODEL), S5_WIDTH ** -0.5),
        'w_branch_pool': nrm(ks[15], (DEPTH, POOL_WIDTH, D_MODEL), POOL_WIDTH ** -0.5),
        'w_branch_attn': nrm(ks[16], (DEPTH, ATTN_WIDTH, D_MODEL), ATTN_WIDTH ** -0.5),
        'w_out': nrm(ks[17], (DEPTH, D_MODEL, D_MODEL), D_MODEL ** -0.5),
        'norm_mlp': 1.0 + nrm(ks[18], (DEPTH, D_MODEL), 0.02),
        'w_up': nrm(ks[19], (DEPTH, D_MODEL, D_FF), D_MODEL ** -0.5),
        'w_down': nrm(ks[20], (DEPTH, D_FF, D_MODEL), D_FF ** -0.5),
        'norm_final': 1.0 + nrm(ks[21], (D_MODEL,), 0.02),
    }


def reference(x, norm_mix, w_in, s5_lam_re, s5_lam_im, s5_log_dt, s5_b_re, s5_b_im,
              s5_c_re, s5_c_im, s5_d, s5_w_glu, pool_w, pool_scale, w_branch_s5,
              w_branch_pool, w_branch_attn, w_out, norm_mlp, w_up, w_down, norm_final):
    bsz, L, _ = x.shape
    split_at = [S5_WIDTH, S5_WIDTH + POOL_WIDTH, S5_WIDTH + POOL_WIDTH + 3 * ATTN_WIDTH]
    h = x
    for l in range(DEPTH):
        xn = rmsnorm(h, norm_mix[l])
        proj = xn @ w_in[l]
        u_s5, u_pool, qkv, gate_logits = jnp.split(proj, split_at, axis=-1)
        y_s5 = s5_mixer(u_s5, s5_lam_re[l], s5_lam_im[l], s5_log_dt[l], s5_b_re[l], s5_b_im[l],
                        s5_c_re[l], s5_c_im[l], s5_d[l], s5_w_glu[l])
        y_pool = pool_mixer(u_pool, pool_w[l], pool_scale[l])
        y_attn = dilated_attention(qkv)
        gates = jax.nn.sigmoid(gate_logits).reshape(bsz, L, N_BRANCHES, D_MODEL)
        merged = (gates[:, :, 0] * (y_s5 @ w_branch_s5[l])
                  + gates[:, :, 1] * (y_pool @ w_branch_pool[l])
                  + gates[:, :, 2] * (y_attn @ w_branch_attn[l]))
        h = h + merged @ w_out[l]
        hn = rmsnorm(h, norm_mlp[l])
        h = h + jnp.square(jax.nn.relu(hn @ w_up[l])) @ w_down[l]
    return rmsnorm(h, norm_final)
```

```python
import functools
import math

import jax
import jax.numpy as jnp
from jax import lax
from jax.experimental import pallas as pl
from jax.experimental.pallas import tpu as pltpu

F32 = jnp.float32
BF16 = jnp.bfloat16

D_MODEL = 1024
S5_WIDTH = 512
S5_GROUP_SIZE = 16
S5_STATE = 64
POOL_WIDTH = 512
POOL_WINDOWS = (2, 4, 8, 16)
POOL_GROUP_SIZE = 128
ATTN_HEADS = 16
HEAD_DIM = 64
ATTN_WIDTH = 1024
ATTN_PATTERNS = ((128, 1), (512, 4), (2048, 16))
D_FF = 4096
NORM_EPS = 1e-6
NEG_INF = -1e30

LANES = 128
SUBLANES = 8
S5_UNIT_GROUPS = LANES // S5_GROUP_SIZE
S5_UNITS = S5_WIDTH // LANES
S5_UNIT_STATE = S5_UNIT_GROUPS * S5_STATE
QBLK = 128
KWIN = 2 * QBLK
VMEM_LIMIT = 56 * 1024 * 1024


def _cparams(*sem):
    return pltpu.CompilerParams(dimension_semantics=sem, vmem_limit_bytes=VMEM_LIMIT)


def _rms(x, g):
    ms = jnp.mean(x * x, axis=-1, keepdims=True)
    return x * lax.rsqrt(ms + NORM_EPS) * g


def _inproj_kernel(x_ref, g_ref, w_ref, us5_ref, upool_ref, q_ref, k_ref, v_ref, gate_ref, xn_ref):
    n = pl.program_id(1)

    @pl.when(n == 0)
    def _():
        xn_ref[...] = _rms(x_ref[...], g_ref[...]).astype(BF16)

    acc = jnp.dot(xn_ref[...], w_ref[...], preferred_element_type=F32)

    @pl.when(n == 0)
    def _():
        us5_ref[...] = acc[:, :S5_WIDTH].astype(BF16)
        upool_ref[...] = acc[:, S5_WIDTH:]

    @pl.when(n == 1)
    def _():
        q_ref[...] = (acc * (HEAD_DIM ** -0.5)).astype(BF16)

    @pl.when(n == 2)
    def _():
        k_ref[...] = acc.astype(BF16)

    @pl.when(n == 3)
    def _():
        v_ref[...] = acc.astype(BF16)

    @pl.when(n >= 4)
    def _():
        gate_ref[...] = jax.nn.sigmoid(acc).astype(BF16)


def _inproj(h2d, g, w_bf16, bsz, seq, tm=512):
    T = bsz * seq
    tiles_per_seq = seq // tm
    n_steps = w_bf16.shape[1] // D_MODEL
    row = lambda m, n: (m, 0)
    return pl.pallas_call(
        _inproj_kernel,
        grid=(T // tm, n_steps),
        in_specs=[
            pl.BlockSpec((tm, D_MODEL), row),
            pl.BlockSpec((1, D_MODEL), lambda m, n: (0, 0)),
            pl.BlockSpec((D_MODEL, D_MODEL), lambda m, n: (0, n)),
        ],
        out_specs=[
            pl.BlockSpec((tm, S5_WIDTH), lambda m, n: (m % tiles_per_seq, m // tiles_per_seq)),
            pl.BlockSpec((tm, POOL_WIDTH), row),
            pl.BlockSpec((tm, ATTN_WIDTH), row),
            pl.BlockSpec((tm, ATTN_WIDTH), row),
            pl.BlockSpec((tm, ATTN_WIDTH), row),
            pl.BlockSpec((tm, D_MODEL), lambda m, n: (m, jnp.clip(n - 4, 0, 2))),
        ],
        out_shape=[
            jax.ShapeDtypeStruct((seq, bsz * S5_WIDTH), BF16),
            jax.ShapeDtypeStruct((T, POOL_WIDTH), F32),
            jax.ShapeDtypeStruct((T, ATTN_WIDTH), BF16),
            jax.ShapeDtypeStruct((T, ATTN_WIDTH), BF16),
            jax.ShapeDtypeStruct((T, ATTN_WIDTH), BF16),
            jax.ShapeDtypeStruct((T, 3 * D_MODEL), BF16),
        ],
        scratch_shapes=[pltpu.VMEM((tm, D_MODEL), BF16)],
        compiler_params=_cparams("parallel", "arbitrary"),
        name="inproj",
    )(h2d, g, w_bf16)


def _s5_kernel(u_ref, bu_ref, cu_ref, a_ref, y_ref, sc_ref, st_ref, *, tt):
    d = pl.program_id(0)
    i = pl.program_id(1)
    uw = 2 * S5_UNIT_STATE
    hs = S5_UNIT_STATE

    @pl.when(i == 0)
    def _():
        st_ref[...] = jnp.zeros_like(st_ref)

    for j in range(S5_UNITS):
        sc_ref[:, uw * j:uw * (j + 1)] = jnp.dot(
            u_ref[:, LANES * j:LANES * (j + 1)], bu_ref[j], preferred_element_type=F32)

    for j in range(S5_UNITS):
        re = slice(uw * j, uw * j + hs)
        im = slice(uw * j + hs, uw * (j + 1))
        ar = jnp.broadcast_to(a_ref[0:1, hs * j:hs * (j + 1)], (SUBLANES, hs))
        ai = jnp.broadcast_to(a_ref[1:2, hs * j:hs * (j + 1)], (SUBLANES, hs))

        def step(s, carry, re=re, im=im, ar=ar, ai=ai):
            sr, si = carry
            t = jnp.where(d == 0, s, tt - 1 - s)
            rows = pl.ds(pl.multiple_of(t * SUBLANES, SUBLANES), SUBLANES)
            nr = ar * sr - ai * si + sc_ref[rows, re]
            ni = ar * si + ai * sr + sc_ref[rows, im]
            sc_ref[rows, re] = nr
            sc_ref[rows, im] = ni
            return nr, ni

        sr, si = lax.fori_loop(0, tt, step, (st_ref[:, re], st_ref[:, im]), unroll=4)
        st_ref[:, re] = sr
        st_ref[:, im] = si

    for j in range(S5_UNITS):
        y_ref[:, LANES * j:LANES * (j + 1)] = jnp.dot(
            sc_ref[:, uw * j:uw * (j + 1)].astype(BF16), cu_ref[j],
            preferred_element_type=F32).astype(BF16)


def _s5_scan(u_tm, bu, cu, a, bsz, seq, tt=128):
    assert bsz == SUBLANES
    n_t = seq // tt
    tile = lambda d, i: i + d * (n_t - 1 - 2 * i)
    ncol = S5_UNITS * 2 * S5_UNIT_STATE
    return pl.pallas_call(
        functools.partial(_s5_kernel, tt=tt),
        grid=(2, n_t),
        in_specs=[
            pl.BlockSpec((tt * bsz, S5_WIDTH), lambda d, i: (tile(d, i), 0)),
            pl.BlockSpec((None, S5_UNITS, LANES, 2 * S5_UNIT_STATE), lambda d, i: (d, 0, 0, 0)),
            pl.BlockSpec((None, S5_UNITS, 2 * S5_UNIT_STATE, LANES), lambda d, i: (d, 0, 0, 0)),
            pl.BlockSpec((None, 2, S5_UNITS * S5_UNIT_STATE), lambda d, i: (d, 0, 0)),
        ],
        out_specs=pl.BlockSpec((None, tt * bsz, S5_WIDTH), lambda d, i: (d, tile(d, i), 0)),
        out_shape=jax.ShapeDtypeStruct((2, seq * bsz, S5_WIDTH), BF16),
        scratch_shapes=[pltpu.VMEM((tt * bsz, ncol), F32), pltpu.VMEM((bsz, ncol), F32)],
        compiler_params=_cparams("arbitrary", "arbitrary"),
        name="s5_scan",
    )(u_tm, bu, cu, a)


def _s5_params(lam_re, lam_im, log_dt, b_re, b_im, c_re, c_im):
    f = lambda t: t.astype(F32)
    lam_re, lam_im, b_re, b_im, c_re, c_im = map(f, (lam_re, lam_im, b_re, b_im, c_re, c_im))
    dt = jnp.exp(f(log_dt))[:, :, None]
    mag = jnp.exp(lam_re * dt)
    a_re = mag * jnp.cos(lam_im * dt)
    a_im = mag * jnp.sin(lam_im * dt)
    den = lam_re * lam_re + lam_im * lam_im
    f_re = ((a_re - 1.0) * lam_re + a_im * lam_im) / den
    f_im = (a_im * lam_re - (a_re - 1.0) * lam_im) / den
    bb_re = f_re[..., None] * b_re - f_im[..., None] * b_im
    bb_im = f_re[..., None] * b_im + f_im[..., None] * b_re
    ug, nu = S5_UNIT_GROUPS, S5_UNITS
    eye = jnp.eye(ug, dtype=F32)

    def pack_b(t):
        t = t.reshape(2, nu, ug, S5_STATE, S5_GROUP_SIZE)
        t = jnp.einsum('dugnp,gh->dugphn', t, eye)
        return t.reshape(2, nu, LANES, S5_UNIT_STATE)

    def pack_c(t):
        t = t.reshape(2, nu, ug, S5_GROUP_SIZE, S5_STATE)
        t = jnp.einsum('dugpn,gh->duhngp', t, eye)
        return t.reshape(2, nu, S5_UNIT_STATE, LANES)

    bu = jnp.concatenate([pack_b(bb_re), pack_b(bb_im)], axis=-1).astype(BF16)
    cu = jnp.concatenate([pack_c(c_re), -pack_c(c_im)], axis=-2).astype(BF16)
    a = jnp.stack([a_re.reshape(2, -1), a_im.reshape(2, -1)], axis=1)
    return bu, cu, a


POOL_PAD = 16


def _pool_kernel(u_ref, w_ref, s_ref, o_ref, xp_ref, *, seq, rows):
    pad = POOL_PAD
    zeros = jnp.zeros((pad, POOL_WIDTH), F32)
    xp_ref[0:pad, :] = zeros
    xp_ref[pad + seq:2 * pad + seq, :] = zeros
    xp_ref[pad:pad + seq, :] = u_ref[...]
    n = rows + 2 * pad

    def chunk(c, carry):
        r0 = pl.multiple_of(c * rows, rows)
        t = r0 + lax.broadcasted_iota(jnp.int32, (rows, 1), 0)
        for g, win in enumerate(POOL_WINDOWS):
            lanes = slice(POOL_GROUP_SIZE * g, POOL_GROUP_SIZE * (g + 1))
            x = xp_ref[pl.ds(r0, n), lanes]
            s = x
            k = 1
            while k < win:
                s = s + pltpu.roll(s, k, axis=0)
                k *= 2
            left = win // 2
            right = win - 1 - left
            if right:
                s = pltpu.roll(s, n - right, axis=0)
            hi = jnp.minimum(t + right + 1, seq)
            lo = jnp.maximum(t - left, 0)
            mean = s[pad:pad + rows] / (hi - lo).astype(F32)
            pooled = mean - x[pad:pad + rows]
            mixed = jnp.dot(pooled.astype(BF16), w_ref[g], preferred_element_type=F32)
            o_ref[pl.ds(r0, rows), lanes] = (mixed * s_ref[:, lanes]).astype(BF16)
        return carry

    lax.fori_loop(0, seq // rows, chunk, 0)


def _pool(u_pool, w_bf16, scale, bsz, seq, rows=256):
    return pl.pallas_call(
        functools.partial(_pool_kernel, seq=seq, rows=rows),
        grid=(bsz,),
        in_specs=[
            pl.BlockSpec((seq, POOL_WIDTH), lambda b: (b, 0)),
            pl.BlockSpec((len(POOL_WINDOWS), POOL_GROUP_SIZE, POOL_GROUP_SIZE), lambda b: (0, 0, 0)),
            pl.BlockSpec((1, POOL_WIDTH), lambda b: (0, 0)),
        ],
        out_specs=pl.BlockSpec((seq, POOL_WIDTH), lambda b: (b, 0)),
        out_shape=jax.ShapeDtypeStruct((bsz * seq, POOL_WIDTH), BF16),
        scratch_shapes=[pltpu.VMEM((seq + 2 * POOL_PAD, POOL_WIDTH), F32)],
        compiler_params=_cparams("parallel"),
        name="pool",
    )(u_pool, w_bf16, scale)


def _attn_kernel(q_ref, k_ref, v_ref, bias_ref, o_ref, lse_ref, *, m_rows, heads):
    hc = pl.program_id(2)

    @pl.when(hc == 0)
    def _():
        lse_ref[...] = jnp.zeros_like(lse_ref)

    lane = lax.broadcasted_iota(jnp.int32, (QBLK, LANES), 1)

    def qblock(qb, carry):
        m0 = pl.multiple_of(qb * QBLK, QBLK)
        ks = pl.multiple_of(jnp.clip(m0 - QBLK // 2, 0, m_rows - KWIN), QBLK // 2)
        var = (m0 - ks) // (QBLK // 2)
        qrows = pl.ds(m0, QBLK)
        krows = pl.ds(ks, KWIN)
        stats = lse_ref[qrows, :]
        for hh in range(heads):
            lanes = slice(HEAD_DIM * hh, HEAD_DIM * (hh + 1))
            hg = hc * heads + hh
            s = lax.dot_general(q_ref[qrows, lanes], k_ref[krows, lanes],
                                (((1,), (1,)), ((), ())), preferred_element_type=F32)
            s = s + bias_ref[var, hg]
            mx = jnp.max(s, axis=-1, keepdims=True)
            p = jnp.exp(s - mx)
            den = jnp.sum(p, axis=-1, keepdims=True)
            num = jnp.dot(p.astype(BF16), v_ref[krows, lanes], preferred_element_type=F32)
            o_ref[qrows, lanes] = (num / den).astype(BF16)
            stats = jnp.where(lane == hg, mx + jnp.log(den), stats)
        lse_ref[qrows, :] = stats
        return carry

    lax.fori_loop(0, m_rows // QBLK, qblock, 0)


def _attn_bias(dil):
    slopes = jnp.exp2(-8.0 * jnp.arange(1, ATTN_HEADS + 1, dtype=F32) / ATTN_HEADS)
    a = jnp.arange(QBLK)[:, None]
    c = jnp.arange(KWIN)[None, :]
    off = (jnp.arange(3) * (QBLK // 2))[:, None, None]
    rel = c[None] - off - a[None]
    dist = (dil * jnp.abs(rel)).astype(F32)
    bias = -slopes[None, :, None, None] * dist[:, None]
    return jnp.where((jnp.abs(rel) <= QBLK // 2)[:, None], bias, NEG_INF)


def _attn_pattern(q, k, v, dil, bsz, seq):
    m_rows = seq // dil
    hw = min(ATTN_WIDTH, 256 * dil)
    nhc = ATTN_WIDTH // hw
    heads = hw // HEAD_DIM
    view = lambda t: t.reshape(bsz, m_rows, dil * ATTN_WIDTH)
    blk = pl.BlockSpec((None, m_rows, hw), lambda b, r, h: (b, 0, r * nhc + h))
    o, lse = pl.pallas_call(
        functools.partial(_attn_kernel, m_rows=m_rows, heads=heads),
        grid=(bsz, dil, nhc),
        in_specs=[blk, blk, blk,
                  pl.BlockSpec((3, ATTN_HEADS, QBLK, KWIN), lambda b, r, h: (0, 0, 0, 0))],
        out_specs=[blk, pl.BlockSpec((None, m_rows, LANES), lambda b, r, h: (b, 0, r))],
        out_shape=[jax.ShapeDtypeStruct((bsz, m_rows, dil * ATTN_WIDTH), BF16),
                   jax.ShapeDtypeStruct((bsz, m_rows, dil * LANES), F32)],
        compiler_params=_cparams("parallel", "parallel", "arbitrary"),
        name=f"attn_d{dil}",
    )(view(q), view(k), view(v), _attn_bias(dil))
    return o.reshape(bsz * seq, ATTN_WIDTH), lse.reshape(bsz * seq, LANES)


def _merge_kernel(yf_ref, yb_ref, us5_ref, dsk_ref, wglu_ref, ypool_ref,
                  o1_ref, o2_ref, o3_ref, l1_ref, l2_ref, l3_ref, exp_ref,
                  gate_ref, h_ref, wbs_ref, wbp_ref, wba_ref, wout_ref, out_ref):
    dot = functools.partial(jnp.dot, preferred_element_type=F32)
    y = (yf_ref[...].astype(F32) + yb_ref[...].astype(F32)
         + dsk_ref[...] * us5_ref[...].astype(F32))
    y = jax.nn.gelu(y)
    y = y * jax.nn.sigmoid(dot(y.astype(BF16), wglu_ref[...]))
    br_s5 = dot(y.astype(BF16), wbs_ref[...])
    br_pool = dot(ypool_ref[...], wbp_ref[...])

    l1, l2, l3 = l1_ref[...], l2_ref[...], l3_ref[...]
    mx = jnp.maximum(jnp.maximum(l1, l2), l3)
    w1, w2, w3 = jnp.exp(l1 - mx), jnp.exp(l2 - mx), jnp.exp(l3 - mx)
    inv = 1.0 / (w1 + w2 + w3)
    expand = lambda w: dot((w * inv).astype(BF16), exp_ref[...])
    att = (expand(w1) * o1_ref[...].astype(F32) + expand(w2) * o2_ref[...].astype(F32)
           + expand(w3) * o3_ref[...].astype(F32))
    br_attn = dot(att.astype(BF16), wba_ref[...])

    merged = (gate_ref[:, 0:D_MODEL].astype(F32) * br_s5
              + gate_ref[:, D_MODEL:2 * D_MODEL].astype(F32) * br_pool
              + gate_ref[:, 2 * D_MODEL:3 * D_MODEL].astype(F32) * br_attn)
    out_ref[...] = h_ref[...] + dot(merged.astype(BF16), wout_ref[...])


def _merge(y_dirs, us5_tm, dsk, wglu, ypool, outs, lses, gates, h2d, wbs, wbp, wba, wout,
           bsz, seq, tm=512):
    T = bsz * seq
    tps = seq // tm
    row = lambda w: pl.BlockSpec((tm, w), lambda b, i: (b * tps + i, 0))
    full = lambda a: pl.BlockSpec(a.shape, lambda b, i: (0,) * a.ndim)
    y_view = y_dirs.reshape(2, seq, bsz * S5_WIDTH)
    ydir = lambda d: pl.BlockSpec((None, tm, S5_WIDTH), lambda b, i: (d, i, b))
    expand = (jnp.arange(LANES)[:, None] == jnp.arange(ATTN_WIDTH)[None, :] // HEAD_DIM).astype(BF16)
    return pl.pallas_call(
        _merge_kernel,
        grid=(bsz, tps),
        in_specs=[ydir(0), ydir(1),
                  pl.BlockSpec((tm, S5_WIDTH), lambda b, i: (i, b)),
                  full(dsk), full(wglu), row(POOL_WIDTH),
                  row(ATTN_WIDTH), row(ATTN_WIDTH), row(ATTN_WIDTH),
                  row(LANES), row(LANES), row(LANES), full(expand),
                  row(3 * D_MODEL), row(D_MODEL),
                  full(wbs), full(wbp), full(wba), full(wout)],
        out_specs=row(D_MODEL),
        out_shape=jax.ShapeDtypeStruct((T, D_MODEL), F32),
        compiler_params=_cparams("parallel", "parallel"),
        name="merge",
    )(y_view, y_view, us5_tm, dsk, wglu, ypool, *outs, *lses, expand, gates, h2d,
      wbs, wbp, wba, wout)


def _mlp_kernel(h_ref, g_ref, wup_ref, wdn_ref, gf_ref, o_ref, *, final, chunk):
    h = h_ref[...]
    hn = _rms(h, g_ref[...]).astype(BF16)
    acc = h
    for c in range(D_FF // chunk):
        cols = slice(chunk * c, chunk * (c + 1))
        up = jnp.dot(hn, wup_ref[:, cols], preferred_element_type=F32)
        act = jnp.square(jnp.maximum(up, 0.0)).astype(BF16)
        acc = acc + jnp.dot(act, wdn_ref[cols, :], preferred_element_type=F32)
    if final:
        acc = _rms(acc, gf_ref[...])
    o_ref[...] = acc


def _mlp(h2d, g, wup, wdn, gf, final, tm=512, chunk=1024):
    T = h2d.shape[0]
    const = lambda a: pl.BlockSpec(a.shape, lambda m: (0,) * a.ndim, pipeline_mode=pl.Buffered(1))
    return pl.pallas_call(
        functools.partial(_mlp_kernel, final=final, chunk=chunk),
        grid=(T // tm,),
        in_specs=[pl.BlockSpec((tm, D_MODEL), lambda m: (m, 0)),
                  const(g), const(wup), const(wdn), const(gf)],
        out_specs=pl.BlockSpec((tm, D_MODEL), lambda m: (m, 0)),
        out_shape=jax.ShapeDtypeStruct((T, D_MODEL), F32),
        compiler_params=_cparams("parallel"),
        name="mlp",
    )(h2d, g, wup, wdn, gf)


def kernel(x, norm_mix, w_in, s5_lam_re, s5_lam_im, s5_log_dt, s5_b_re, s5_b_im, s5_c_re, s5_c_im, s5_d, s5_w_glu, pool_w, pool_scale, w_branch_s5, w_branch_pool, w_branch_attn, w_out, norm_mlp, w_up, w_down, norm_final):
    bsz, seq, _ = x.shape
    depth = w_in.shape[0]
    bf = lambda t: t.astype(BF16)
    row = lambda t: t.astype(F32).reshape(1, -1)
    h = x.astype(F32).reshape(bsz * seq, D_MODEL)
    for l in range(depth):
        us5_tm, u_pool, q, k, v, gates = _inproj(h, row(norm_mix[l]), bf(w_in[l]), bsz, seq)
        bu, cu, a = _s5_params(s5_lam_re[l], s5_lam_im[l], s5_log_dt[l], s5_b_re[l], s5_b_im[l],
                               s5_c_re[l], s5_c_im[l])
        y_dirs = _s5_scan(us5_tm.reshape(seq * bsz, S5_WIDTH), bu, cu, a, bsz, seq)
        y_pool = _pool(u_pool, bf(pool_w[l]), row(pool_scale[l]), bsz, seq)
        outs, lses = zip(*[_attn_pattern(q, k, v, dil, bsz, seq) for _, dil in ATTN_PATTERNS])
        h = _merge(y_dirs, us5_tm, row(s5_d[l]), bf(s5_w_glu[l]), y_pool, outs, lses, gates, h,
                   bf(w_branch_s5[l]), bf(w_branch_pool[l]), bf(w_branch_attn[l]), bf(w_out[l]),
                   bsz, seq)
        h = _mlp(h, row(norm_mlp[l]), bf(w_up[l]), bf(w_down[l]), row(norm_final), l == depth - 1)
    return h.reshape(bsz, seq, D_MODEL).astype(x.dtype)
```

```python
import functools
import math

import jax
import jax.numpy as jnp
from jax import lax
from jax.experimental import pallas as pl
from jax.experimental.pallas import tpu as pltpu

F32 = jnp.float32
BF16 = jnp.bfloat16

D_MODEL = 1024
S5_WIDTH = 512
S5_GROUP_SIZE = 16
S5_STATE = 64
POOL_WIDTH = 512
POOL_WINDOWS = (2, 4, 8, 16)
POOL_GROUP_SIZE = 128
ATTN_HEADS = 16
HEAD_DIM = 64
ATTN_WIDTH = 1024
ATTN_PATTERNS = ((128, 1), (512, 4), (2048, 16))
D_FF = 4096
NORM_EPS = 1e-6
NEG_INF = -1e30

LANES = 128
SUBLANES = 8
S5_UNIT_GROUPS = LANES // S5_GROUP_SIZE
S5_UNITS = S5_WIDTH // LANES
S5_UNIT_STATE = S5_UNIT_GROUPS * S5_STATE
QBLK = 128
KWIN = 2 * QBLK
VMEM_LIMIT = 56 * 1024 * 1024


def _cparams(*sem):
    return pltpu.CompilerParams(dimension_semantics=sem, vmem_limit_bytes=VMEM_LIMIT)


def _rms(x, g):
    ms = jnp.mean(x * x, axis=-1, keepdims=True)
    return x * lax.rsqrt(ms + NORM_EPS) * g


DILATIONS = tuple(dil for _, dil in ATTN_PATTERNS)
LOG2E = math.log2(math.e)
Q_SCALE = HEAD_DIM ** -0.5 * LOG2E


def _inproj_kernel(x_ref, g_ref, w_ref, us5_ref, upool_ref, *rest, tm):
    nd = len(DILATIONS)
    qkv_refs = [rest[nd * i:nd * (i + 1)] for i in range(3)]
    gate_ref, lay_ref = rest[3 * nd], rest[3 * nd + 1]
    xn = _rms(x_ref[...], g_ref[...]).astype(BF16)
    proj = lambda c: jnp.dot(xn, w_ref[:, D_MODEL * c:D_MODEL * (c + 1)], preferred_element_type=F32)

    acc = proj(0)
    us5_ref[...] = acc[:, :S5_WIDTH].astype(BF16)
    upool_ref[...] = acc[:, S5_WIDTH:]

    for i, refs in enumerate(qkv_refs):
        acc = proj(1 + i)
        if i == 0:
            acc = acc * Q_SCALE
        for j in range(ATTN_WIDTH // LANES):
            lay_ref[j] = acc[:, LANES * j:LANES * (j + 1)]
        for dil, ref in zip(DILATIONS, refs):
            if dil == 1:
                ref[0] = acc.astype(BF16)
                continue
            for r in range(dil):
                for j in range(ATTN_WIDTH // LANES):
                    ref[r, :, LANES * j:LANES * (j + 1)] = (
                        lay_ref[j, pl.ds(r, tm // dil, stride=dil), :].astype(BF16))

    for c in range(3):
        gate_ref[:, D_MODEL * c:D_MODEL * (c + 1)] = jax.nn.sigmoid(proj(4 + c)).astype(BF16)


def _inproj(h2d, g, w_bf16, bsz, seq, tm=512):
    T = bsz * seq
    tps = seq // tm
    row = lambda m: (m, 0)
    const = lambda a: pl.BlockSpec(a.shape, lambda m: (0,) * a.ndim, pipeline_mode=pl.Buffered(1))
    lay_specs = [pl.BlockSpec((None, dil, tm // dil, ATTN_WIDTH), lambda m: (m // tps, 0, m % tps, 0))
                 for dil in DILATIONS]
    lay_shapes = [jax.ShapeDtypeStruct((bsz, dil, seq // dil, ATTN_WIDTH), BF16) for dil in DILATIONS]
    nd = len(DILATIONS)
    outs = pl.pallas_call(
        functools.partial(_inproj_kernel, tm=tm),
        grid=(T // tm,),
        in_specs=[pl.BlockSpec((tm, D_MODEL), row), const(g), const(w_bf16)],
        out_specs=[
            pl.BlockSpec((tm, S5_WIDTH), lambda m: (m % tps, m // tps)),
            pl.BlockSpec((tm, POOL_WIDTH), row),
            *(lay_specs * 3),
            pl.BlockSpec((tm, 3 * D_MODEL), row),
        ],
        out_shape=[
            jax.ShapeDtypeStruct((seq, bsz * S5_WIDTH), BF16),
            jax.ShapeDtypeStruct((T, POOL_WIDTH), F32),
            *(lay_shapes * 3),
            jax.ShapeDtypeStruct((T, 3 * D_MODEL), BF16),
        ],
        scratch_shapes=[pltpu.VMEM((ATTN_WIDTH // LANES, tm, LANES), F32)],
        compiler_params=_cparams("parallel"),
        name="inproj",
    )(h2d, g, w_bf16)
    q, k, v = (outs[2 + nd * i:2 + nd * (i + 1)] for i in range(3))
    return outs[0], outs[1], q, k, v, outs[2 + 3 * nd]


def _s5_kernel(u_ref, bu_ref, cu_ref, a_ref, y_ref, sc_ref, st_ref, *, tt):
    d = pl.program_id(0)
    i = pl.program_id(1)
    uw = 2 * S5_UNIT_STATE
    hs = S5_UNIT_STATE

    @pl.when(i == 0)
    def _():
        st_ref[...] = jnp.zeros_like(st_ref)

    for j in range(S5_UNITS):
        sc_ref[:, uw * j:uw * (j + 1)] = jnp.dot(
            u_ref[:, LANES * j:LANES * (j + 1)], bu_ref[j], preferred_element_type=F32)

    for j in range(S5_UNITS):
        re = slice(uw * j, uw * j + hs)
        im = slice(uw * j + hs, uw * (j + 1))
        ar = jnp.broadcast_to(a_ref[0:1, hs * j:hs * (j + 1)], (SUBLANES, hs))
        ai = jnp.broadcast_to(a_ref[1:2, hs * j:hs * (j + 1)], (SUBLANES, hs))

        def step(s, carry, re=re, im=im, ar=ar, ai=ai):
            sr, si = carry
            t = jnp.where(d == 0, s, tt - 1 - s)
            rows = pl.ds(pl.multiple_of(t * SUBLANES, SUBLANES), SUBLANES)
            nr = ar * sr - ai * si + sc_ref[rows, re]
            ni = ar * si + ai * sr + sc_ref[rows, im]
            sc_ref[rows, re] = nr
            sc_ref[rows, im] = ni
            return nr, ni

        sr, si = lax.fori_loop(0, tt, step, (st_ref[:, re], st_ref[:, im]), unroll=4)
        st_ref[:, re] = sr
        st_ref[:, im] = si

    for j in range(S5_UNITS):
        y_ref[:, LANES * j:LANES * (j + 1)] = jnp.dot(
            sc_ref[:, uw * j:uw * (j + 1)].astype(BF16), cu_ref[j],
            preferred_element_type=F32).astype(BF16)


def _s5_scan(u_tm, bu, cu, a, bsz, seq, tt=128):
    assert bsz == SUBLANES
    n_t = seq // tt
    tile = lambda d, i: i + d * (n_t - 1 - 2 * i)
    ncol = S5_UNITS * 2 * S5_UNIT_STATE
    return pl.pallas_call(
        functools.partial(_s5_kernel, tt=tt),
        grid=(2, n_t),
        in_specs=[
            pl.BlockSpec((tt * bsz, S5_WIDTH), lambda d, i: (tile(d, i), 0)),
            pl.BlockSpec((None, S5_UNITS, LANES, 2 * S5_UNIT_STATE), lambda d, i: (d, 0, 0, 0)),
            pl.BlockSpec((None, S5_UNITS, 2 * S5_UNIT_STATE, LANES), lambda d, i: (d, 0, 0, 0)),
            pl.BlockSpec((None, 2, S5_UNITS * S5_UNIT_STATE), lambda d, i: (d, 0, 0)),
        ],
        out_specs=pl.BlockSpec((None, tt * bsz, S5_WIDTH), lambda d, i: (d, tile(d, i), 0)),
        out_shape=jax.ShapeDtypeStruct((2, seq * bsz, S5_WIDTH), BF16),
        scratch_shapes=[pltpu.VMEM((tt * bsz, ncol), F32), pltpu.VMEM((bsz, ncol), F32)],
        compiler_params=_cparams("arbitrary", "arbitrary"),
        name="s5_scan",
    )(u_tm, bu, cu, a)


def _s5_params(lam_re, lam_im, log_dt, b_re, b_im, c_re, c_im):
    f = lambda t: t.astype(F32)
    lam_re, lam_im, b_re, b_im, c_re, c_im = map(f, (lam_re, lam_im, b_re, b_im, c_re, c_im))
    dt = jnp.exp(f(log_dt))[:, :, None]
    mag = jnp.exp(lam_re * dt)
    a_re = mag * jnp.cos(lam_im * dt)
    a_im = mag * jnp.sin(lam_im * dt)
    den = lam_re * lam_re + lam_im * lam_im
    f_re = ((a_re - 1.0) * lam_re + a_im * lam_im) / den
    f_im = (a_im * lam_re - (a_re - 1.0) * lam_im) / den
    bb_re = f_re[..., None] * b_re - f_im[..., None] * b_im
    bb_im = f_re[..., None] * b_im + f_im[..., None] * b_re
    ug, nu = S5_UNIT_GROUPS, S5_UNITS
    eye = jnp.eye(ug, dtype=F32)

    def pack_b(t):
        t = t.reshape(2, nu, ug, S5_STATE, S5_GROUP_SIZE)
        t = jnp.einsum('dugnp,gh->dugphn', t, eye)
        return t.reshape(2, nu, LANES, S5_UNIT_STATE)

    def pack_c(t):
        t = t.reshape(2, nu, ug, S5_GROUP_SIZE, S5_STATE)
        t = jnp.einsum('dugpn,gh->duhngp', t, eye)
        return t.reshape(2, nu, S5_UNIT_STATE, LANES)

    bu = jnp.concatenate([pack_b(bb_re), pack_b(bb_im)], axis=-1).astype(BF16)
    cu = jnp.concatenate([pack_c(c_re), -pack_c(c_im)], axis=-2).astype(BF16)
    a = jnp.stack([a_re.reshape(2, -1), a_im.reshape(2, -1)], axis=1)
    return bu, cu, a


POOL_PAD = 16


def _pool_kernel(u_ref, w_ref, s_ref, o_ref, xp_ref, *, seq, rows):
    pad = POOL_PAD
    zeros = jnp.zeros((pad, POOL_WIDTH), F32)
    xp_ref[0:pad, :] = zeros
    xp_ref[pad + seq:2 * pad + seq, :] = zeros
    xp_ref[pad:pad + seq, :] = u_ref[...]
    n = rows + 2 * pad

    def chunk(c, carry):
        r0 = pl.multiple_of(c * rows, rows)
        t = r0 + lax.broadcasted_iota(jnp.int32, (rows, 1), 0)
        for g, win in enumerate(POOL_WINDOWS):
            lanes = slice(POOL_GROUP_SIZE * g, POOL_GROUP_SIZE * (g + 1))
            x = xp_ref[pl.ds(r0, n), lanes]
            s = x
            k = 1
            while k < win:
                s = s + pltpu.roll(s, k, axis=0)
                k *= 2
            left = win // 2
            right = win - 1 - left
            if right:
                s = pltpu.roll(s, n - right, axis=0)
            hi = jnp.minimum(t + right + 1, seq)
            lo = jnp.maximum(t - left, 0)
            mean = s[pad:pad + rows] / (hi - lo).astype(F32)
            pooled = mean - x[pad:pad + rows]
            mixed = jnp.dot(pooled.astype(BF16), w_ref[g], preferred_element_type=F32)
            o_ref[pl.ds(r0, rows), lanes] = (mixed * s_ref[:, lanes]).astype(BF16)
        return carry

    lax.fori_loop(0, seq // rows, chunk, 0)


def _pool(u_pool, w_bf16, scale, bsz, seq, rows=256):
    return pl.pallas_call(
        functools.partial(_pool_kernel, seq=seq, rows=rows),
        grid=(bsz,),
        in_specs=[
            pl.BlockSpec((seq, POOL_WIDTH), lambda b: (b, 0)),
            pl.BlockSpec((len(POOL_WINDOWS), POOL_GROUP_SIZE, POOL_GROUP_SIZE), lambda b: (0, 0, 0)),
            pl.BlockSpec((1, POOL_WIDTH), lambda b: (0, 0)),
        ],
        out_specs=pl.BlockSpec((seq, POOL_WIDTH), lambda b: (b, 0)),
        out_shape=jax.ShapeDtypeStruct((bsz * seq, POOL_WIDTH), BF16),
        scratch_shapes=[pltpu.VMEM((seq + 2 * POOL_PAD, POOL_WIDTH), F32)],
        compiler_params=_cparams("parallel"),
        name="pool",
    )(u_pool, w_bf16, scale)


def _stat_lane(head):
    return head + HEAD_DIM * (1 - head % 2)


def _attn_kernel(q_ref, k_ref, v_ref, bias_ref, o_ref, mx_ref, den_ref, *osc, dil, m_rows, pairs, unroll):
    hc = pl.program_id(1)

    @pl.when(hc == 0)
    def _():
        mx_ref[...] = jnp.zeros_like(mx_ref)
        den_ref[...] = jnp.ones_like(den_ref)

    lane = lax.broadcasted_iota(jnp.int32, (QBLK, LANES), 1)
    low = lane < HEAD_DIM
    low_k = lax.broadcasted_iota(jnp.int32, (KWIN, LANES), 1) < HEAD_DIM
    ones = jnp.ones((KWIN, LANES), BF16)
    nqb = m_rows // QBLK

    def qblock(it, carry):
        r = it // nqb
        m0 = pl.multiple_of((it % nqb) * QBLK, QBLK)
        ks = pl.multiple_of(jnp.clip(m0 - QBLK // 2, 0, m_rows - KWIN), QBLK // 2)
        var = (m0 - ks) // (QBLK // 2)
        qrows = pl.ds(m0, QBLK)
        krows = pl.ds(ks, KWIN)
        nat = qrows if dil == 1 else pl.ds(r + dil * m0, QBLK, stride=dil)
        mxs = mx_ref[nat, :]
        dens = den_ref[nat, :]
        for j in range(pairs):
            lanes = slice(LANES * j, LANES * (j + 1))
            qp = q_ref[r, qrows, lanes]
            kp = k_ref[r, krows, lanes]
            vp = v_ref[r, krows, lanes]
            zero = jnp.zeros_like(qp)
            q2 = jnp.concatenate([jnp.where(low, qp, zero), jnp.where(low, zero, qp)], axis=0)
            s2 = lax.dot_general(q2, kp, (((1,), (1,)), ((), ())), preferred_element_type=F32)
            vsel = (jnp.where(low_k, vp, ones), jnp.where(low_k, ones, vp))
            halves = []
            for e in range(2):
                hg = 2 * (hc * pairs + j) + e
                s = s2[QBLK * e:QBLK * (e + 1)] + bias_ref[var, hg]
                mx = jnp.max(s, axis=-1, keepdims=True)
                p = jnp.exp2(s - mx).astype(BF16)
                pv = jnp.dot(p, vsel[e], preferred_element_type=F32)
                halves.append(pv)
                mxs = jnp.where(lane == _stat_lane(hg), mx, mxs)
                dens = jnp.where(lane == _stat_lane(hg), pv, dens)
            o_pair = jnp.where(low, halves[0], halves[1])
            if dil == 1:
                o_ref[qrows, lanes] = o_pair.astype(BF16)
            else:
                osc[0][j, nat, :] = o_pair
        mx_ref[nat, :] = mxs
        den_ref[nat, :] = dens
        return carry

    lax.fori_loop(0, dil * nqb, qblock, 0, unroll=unroll)
    if dil > 1:
        for j in range(pairs):
            o_ref[:, LANES * j:LANES * (j + 1)] = osc[0][j].astype(BF16)


def _attn_bias(dil):
    slopes = jnp.exp2(-8.0 * jnp.arange(1, ATTN_HEADS + 1, dtype=F32) / ATTN_HEADS)
    a = jnp.arange(QBLK)[:, None]
    c = jnp.arange(KWIN)[None, :]
    off = (jnp.arange(3) * (QBLK // 2))[:, None, None]
    rel = c[None] - off - a[None]
    dist = (dil * jnp.abs(rel)).astype(F32)
    bias = -LOG2E * slopes[None, :, None, None] * dist[:, None]
    return jnp.where((jnp.abs(rel) <= QBLK // 2)[:, None], bias, NEG_INF)


def _attn_pattern(q, k, v, dil, bsz, seq, hw=256, unroll=4):
    m_rows = seq // dil
    nhc = ATTN_WIDTH // hw
    pairs = hw // LANES
    blk = pl.BlockSpec((None, dil, m_rows, hw), lambda b, h: (b, 0, 0, h))
    stat = pl.BlockSpec((seq, LANES), lambda b, h: (b, 0))
    bias = _attn_bias(dil)
    return pl.pallas_call(
        functools.partial(_attn_kernel, dil=dil, m_rows=m_rows, pairs=pairs, unroll=unroll),
        grid=(bsz, nhc),
        in_specs=[blk, blk, blk,
                  pl.BlockSpec(bias.shape, lambda b, h: (0, 0, 0, 0), pipeline_mode=pl.Buffered(1))],
        out_specs=[pl.BlockSpec((seq, hw), lambda b, h: (b, h)), stat, stat],
        out_shape=[jax.ShapeDtypeStruct((bsz * seq, ATTN_WIDTH), BF16),
                   jax.ShapeDtypeStruct((bsz * seq, LANES), F32),
                   jax.ShapeDtypeStruct((bsz * seq, LANES), F32)],
        scratch_shapes=[pltpu.VMEM((pairs, seq, LANES), F32)] if dil > 1 else [],
        compiler_params=_cparams("parallel", "arbitrary"),
        name=f"attn_d{dil}",
    )(q, k, v, bias)


def _merge_kernel(yf_ref, yb_ref, us5_ref, dsk_ref, wglu_ref, ypool_ref,
                  o1_ref, o2_ref, o3_ref, m1_ref, m2_ref, m3_ref, d1_ref, d2_ref, d3_ref, exp_ref,
                  gate_ref, h_ref, wbs_ref, wbp_ref, wba_ref, wout_ref, out_ref):
    dot = functools.partial(jnp.dot, preferred_element_type=F32)
    y = (yf_ref[...].astype(F32) + yb_ref[...].astype(F32)
         + dsk_ref[...] * us5_ref[...].astype(F32))
    y = jax.nn.gelu(y)
    y = y * jax.nn.sigmoid(dot(y.astype(BF16), wglu_ref[...]))
    br_s5 = dot(y.astype(BF16), wbs_ref[...])
    br_pool = dot(ypool_ref[...], wbp_ref[...])

    m1, m2, m3 = m1_ref[...], m2_ref[...], m3_ref[...]
    mx = jnp.maximum(jnp.maximum(m1, m2), m3)
    w1, w2, w3 = jnp.exp2(m1 - mx), jnp.exp2(m2 - mx), jnp.exp2(m3 - mx)
    inv = 1.0 / (w1 * d1_ref[...] + w2 * d2_ref[...] + w3 * d3_ref[...])
    expand = lambda w: dot((w * inv).astype(BF16), exp_ref[...])
    att = (expand(w1) * o1_ref[...].astype(F32) + expand(w2) * o2_ref[...].astype(F32)
           + expand(w3) * o3_ref[...].astype(F32))
    br_attn = dot(att.astype(BF16), wba_ref[...])

    merged = (gate_ref[:, 0:D_MODEL].astype(F32) * br_s5
              + gate_ref[:, D_MODEL:2 * D_MODEL].astype(F32) * br_pool
              + gate_ref[:, 2 * D_MODEL:3 * D_MODEL].astype(F32) * br_attn)
    out_ref[...] = h_ref[...] + dot(merged.astype(BF16), wout_ref[...])


def _merge(y_dirs, us5_tm, dsk, wglu, ypool, outs, maxes, dens, gates, h2d, wbs, wbp, wba, wout,
           bsz, seq, tm=512):
    T = bsz * seq
    tps = seq // tm
    row = lambda w: pl.BlockSpec((tm, w), lambda b, i: (b * tps + i, 0))
    full = lambda a: pl.BlockSpec(a.shape, lambda b, i: (0,) * a.ndim)
    y_view = y_dirs.reshape(2, seq, bsz * S5_WIDTH)
    ydir = lambda d: pl.BlockSpec((None, tm, S5_WIDTH), lambda b, i: (d, i, b))
    head = jnp.arange(ATTN_WIDTH)[None, :] // HEAD_DIM
    expand = (jnp.arange(LANES)[:, None] == _stat_lane(head)).astype(BF16)
    return pl.pallas_call(
        _merge_kernel,
        grid=(bsz, tps),
        in_specs=[ydir(0), ydir(1),
                  pl.BlockSpec((tm, S5_WIDTH), lambda b, i: (i, b)),
                  full(dsk), full(wglu), row(POOL_WIDTH),
                  row(ATTN_WIDTH), row(ATTN_WIDTH), row(ATTN_WIDTH),
                  *([row(LANES)] * 6), full(expand),
                  row(3 * D_MODEL), row(D_MODEL),
                  full(wbs), full(wbp), full(wba), full(wout)],
        out_specs=row(D_MODEL),
        out_shape=jax.ShapeDtypeStruct((T, D_MODEL), F32),
        compiler_params=_cparams("parallel", "parallel"),
        name="merge",
    )(y_view, y_view, us5_tm, dsk, wglu, ypool, *outs, *maxes, *dens, expand, gates, h2d,
      wbs, wbp, wba, wout)


def _mlp_kernel(h_ref, g_ref, wup_ref, wdn_ref, gf_ref, o_ref, *, final, chunk):
    h = h_ref[...]
    hn = _rms(h, g_ref[...]).astype(BF16)
    acc = h
    for c in range(D_FF // chunk):
        cols = slice(chunk * c, chunk * (c + 1))
        up = jnp.dot(hn, wup_ref[:, cols], preferred_element_type=F32)
        act = jnp.square(jnp.maximum(up, 0.0)).astype(BF16)
        acc = acc + jnp.dot(act, wdn_ref[cols, :], preferred_element_type=F32)
    if final:
        acc = _rms(acc, gf_ref[...])
    o_ref[...] = acc


def _mlp(h2d, g, wup, wdn, gf, final, tm=512, chunk=1024):
    T = h2d.shape[0]
    const = lambda a: pl.BlockSpec(a.shape, lambda m: (0,) * a.ndim, pipeline_mode=pl.Buffered(1))
    return pl.pallas_call(
        functools.partial(_mlp_kernel, final=final, chunk=chunk),
        grid=(T // tm,),
        in_specs=[pl.BlockSpec((tm, D_MODEL), lambda m: (m, 0)),
                  const(g), const(wup), const(wdn), const(gf)],
        out_specs=pl.BlockSpec((tm, D_MODEL), lambda m: (m, 0)),
        out_shape=jax.ShapeDtypeStruct((T, D_MODEL), F32),
        compiler_params=_cparams("parallel"),
        name="mlp",
    )(h2d, g, wup, wdn, gf)


def kernel(x, norm_mix, w_in, s5_lam_re, s5_lam_im, s5_log_dt, s5_b_re, s5_b_im, s5_c_re, s5_c_im, s5_d, s5_w_glu, pool_w, pool_scale, w_branch_s5, w_branch_pool, w_branch_attn, w_out, norm_mlp, w_up, w_down, norm_final):
    bsz, seq, _ = x.shape
    depth = w_in.shape[0]
    bf = lambda t: t.astype(BF16)
    row = lambda t: t.astype(F32).reshape(1, -1)
    h = x.astype(F32).reshape(bsz * seq, D_MODEL)
    for l in range(depth):
        us5_tm, u_pool, q, k, v, gates = _inproj(h, row(norm_mix[l]), bf(w_in[l]), bsz, seq)
        bu, cu, a = _s5_params(s5_lam_re[l], s5_lam_im[l], s5_log_dt[l], s5_b_re[l], s5_b_im[l],
                               s5_c_re[l], s5_c_im[l])
        y_dirs = _s5_scan(us5_tm.reshape(seq * bsz, S5_WIDTH), bu, cu, a, bsz, seq)
        y_pool = _pool(u_pool, bf(pool_w[l]), row(pool_scale[l]), bsz, seq)
        outs, maxes, dens = zip(*[_attn_pattern(q[i], k[i], v[i], dil, bsz, seq)
                                  for i, dil in enumerate(DILATIONS)])
        h = _merge(y_dirs, us5_tm, row(s5_d[l]), bf(s5_w_glu[l]), y_pool, outs, maxes, dens, gates, h,
                   bf(w_branch_s5[l]), bf(w_branch_pool[l]), bf(w_branch_attn[l]), bf(w_out[l]),
                   bsz, seq)
        h = _mlp(h, row(norm_mlp[l]), bf(w_up[l]), bf(w_down[l]), row(norm_final), l == depth - 1)
    return h.reshape(bsz, seq, D_MODEL).astype(x.dtype)
```

```python
import functools
import math

import jax
import jax.numpy as jnp
from jax import lax
from jax.experimental import pallas as pl
from jax.experimental.pallas import tpu as pltpu

F32 = jnp.float32
BF16 = jnp.bfloat16

D_MODEL = 1024
S5_WIDTH = 512
S5_GROUP_SIZE = 16
S5_STATE = 64
POOL_WIDTH = 512
POOL_WINDOWS = (2, 4, 8, 16)
POOL_GROUP_SIZE = 128
ATTN_HEADS = 16
HEAD_DIM = 64
ATTN_WIDTH = 1024
ATTN_PATTERNS = ((128, 1), (512, 4), (2048, 16))
D_FF = 4096
NORM_EPS = 1e-6
NEG_INF = -1e30

LANES = 128
SUBLANES = 8
S5_UNIT_GROUPS = LANES // S5_GROUP_SIZE
S5_UNITS = S5_WIDTH // LANES
S5_UNIT_STATE = S5_UNIT_GROUPS * S5_STATE
QBLK = 128
KWIN = 2 * QBLK
ATTN_CHUNK = 256
ATTN_NCHUNK = ATTN_WIDTH // ATTN_CHUNK
VMEM_LIMIT = 56 * 1024 * 1024


def _cparams(*sem):
    return pltpu.CompilerParams(dimension_semantics=sem, vmem_limit_bytes=VMEM_LIMIT)


def _rms(x, g):
    ms = jnp.mean(x * x, axis=-1, keepdims=True)
    return x * lax.rsqrt(ms + NORM_EPS) * g


DILATIONS = tuple(dil for _, dil in ATTN_PATTERNS)
LOG2E = math.log2(math.e)
Q_SCALE = HEAD_DIM ** -0.5 * LOG2E


def _inproj_kernel(x_ref, g_ref, w_ref, us5_ref, upool_ref, *rest, tm):
    nd = len(DILATIONS)
    qkv_refs = [rest[nd * i:nd * (i + 1)] for i in range(3)]
    gate_ref = rest[3 * nd]
    lay_refs = dict(zip(DILATIONS[:-1], rest[3 * nd + 1:]))
    xn = _rms(x_ref[...], g_ref[...]).astype(BF16)
    proj = lambda c: jnp.dot(xn, w_ref[:, D_MODEL * c:D_MODEL * (c + 1)], preferred_element_type=F32)

    acc = proj(0)
    us5_ref[...] = acc[:, :S5_WIDTH].astype(BF16)
    upool_ref[...] = acc[:, S5_WIDTH:]

    for i, refs in enumerate(qkv_refs):
        acc = proj(1 + i)
        if i == 0:
            acc = acc * Q_SCALE
        refs[0][0] = acc.astype(BF16)
        for j in range(ATTN_WIDTH // LANES):
            lay_refs[1][j] = acc[:, LANES * j:LANES * (j + 1)]
        for base, dil, ref in zip(DILATIONS[:-1], DILATIONS[1:], refs[1:]):
            ratio, rows = dil // base, tm // dil
            for c in range(base):
                for a in range(ratio):
                    r = c + base * a
                    for j in range(ATTN_WIDTH // LANES):
                        piece = lay_refs[base][j, pl.ds(c * (tm // base) + a, rows, stride=ratio), :]
                        ref[r, :, LANES * j:LANES * (j + 1)] = piece.astype(BF16)
                        if dil in lay_refs:
                            lay_refs[dil][j, r * rows:(r + 1) * rows, :] = piece

    for c in range(3):
        gate_ref[:, D_MODEL * c:D_MODEL * (c + 1)] = jax.nn.sigmoid(proj(4 + c)).astype(BF16)


def _inproj(h2d, g, w_bf16, bsz, seq, tm=512):
    T = bsz * seq
    tps = seq // tm
    row = lambda m: (m, 0)
    const = lambda a: pl.BlockSpec(a.shape, lambda m: (0,) * a.ndim, pipeline_mode=pl.Buffered(1))
    lay_specs = [pl.BlockSpec((None, dil, tm // dil, ATTN_WIDTH), lambda m: (m // tps, 0, m % tps, 0))
                 for dil in DILATIONS]
    lay_shapes = [jax.ShapeDtypeStruct((bsz, dil, seq // dil, ATTN_WIDTH), BF16) for dil in DILATIONS]
    nd = len(DILATIONS)
    outs = pl.pallas_call(
        functools.partial(_inproj_kernel, tm=tm),
        grid=(T // tm,),
        in_specs=[pl.BlockSpec((tm, D_MODEL), row), const(g), const(w_bf16)],
        out_specs=[
            pl.BlockSpec((tm, S5_WIDTH), lambda m: (m % tps, m // tps)),
            pl.BlockSpec((tm, POOL_WIDTH), row),
            *(lay_specs * 3),
            pl.BlockSpec((tm, 3 * D_MODEL), row),
        ],
        out_shape=[
            jax.ShapeDtypeStruct((seq, bsz * S5_WIDTH), BF16),
            jax.ShapeDtypeStruct((T, POOL_WIDTH), F32),
            *(lay_shapes * 3),
            jax.ShapeDtypeStruct((T, 3 * D_MODEL), BF16),
        ],
        scratch_shapes=[pltpu.VMEM((ATTN_WIDTH // LANES, tm, LANES), F32)] * (nd - 1),
        compiler_params=_cparams("parallel"),
        name="inproj",
    )(h2d, g, w_bf16)
    q, k, v = (outs[2 + nd * i:2 + nd * (i + 1)] for i in range(3))
    return outs[0], outs[1], q, k, v, outs[2 + 3 * nd]


def _s5_kernel(u_ref, bu_ref, cu_ref, a_ref, y_ref, sc_ref, st_ref, io_ref, *, tt, bsz):
    d = pl.program_id(0)
    i = pl.program_id(1)
    uw = 2 * S5_UNIT_STATE
    hs = S5_UNIT_STATE

    @pl.when(i == 0)
    def _():
        st_ref[...] = jnp.zeros_like(st_ref)

    for b in range(bsz):
        for j in range(S5_UNITS):
            lanes = slice(S5_WIDTH * b + LANES * j, S5_WIDTH * b + LANES * (j + 1))
            io_ref[j, pl.ds(b, tt, stride=bsz), :] = u_ref[:, lanes].astype(F32)

    for j in range(S5_UNITS):
        sc_ref[:, uw * j:uw * (j + 1)] = jnp.dot(
            io_ref[j].astype(BF16), bu_ref[j], preferred_element_type=F32)

    for j in range(S5_UNITS):
        re = slice(uw * j, uw * j + hs)
        im = slice(uw * j + hs, uw * (j + 1))
        ar = jnp.broadcast_to(a_ref[0:1, hs * j:hs * (j + 1)], (SUBLANES, hs))
        ai = jnp.broadcast_to(a_ref[1:2, hs * j:hs * (j + 1)], (SUBLANES, hs))

        def step(s, carry, re=re, im=im, ar=ar, ai=ai):
            sr, si = carry
            t = jnp.where(d == 0, s, tt - 1 - s)
            rows = pl.ds(pl.multiple_of(t * SUBLANES, SUBLANES), SUBLANES)
            nr = ar * sr - ai * si + sc_ref[rows, re]
            ni = ar * si + ai * sr + sc_ref[rows, im]
            sc_ref[rows, re] = nr
            sc_ref[rows, im] = ni
            return nr, ni

        sr, si = lax.fori_loop(0, tt, step, (st_ref[:, re], st_ref[:, im]), unroll=4)
        st_ref[:, re] = sr
        st_ref[:, im] = si

    for j in range(S5_UNITS):
        io_ref[j] = jnp.dot(sc_ref[:, uw * j:uw * (j + 1)].astype(BF16), cu_ref[j],
                            preferred_element_type=F32)
    for b in range(bsz):
        for j in range(S5_UNITS):
            lanes = slice(S5_WIDTH * b + LANES * j, S5_WIDTH * b + LANES * (j + 1))
            y_ref[:, lanes] = io_ref[j, pl.ds(b, tt, stride=bsz), :].astype(BF16)


def _s5_scan(u_tm, bu, cu, a, bsz, seq, tt=128):
    assert bsz == SUBLANES
    n_t = seq // tt
    tile = lambda d, i: i + d * (n_t - 1 - 2 * i)
    ncol = S5_UNITS * 2 * S5_UNIT_STATE
    return pl.pallas_call(
        functools.partial(_s5_kernel, tt=tt, bsz=bsz),
        grid=(2, n_t),
        in_specs=[
            pl.BlockSpec((tt, bsz * S5_WIDTH), lambda d, i: (tile(d, i), 0)),
            pl.BlockSpec((None, S5_UNITS, LANES, 2 * S5_UNIT_STATE), lambda d, i: (d, 0, 0, 0)),
            pl.BlockSpec((None, S5_UNITS, 2 * S5_UNIT_STATE, LANES), lambda d, i: (d, 0, 0, 0)),
            pl.BlockSpec((None, 2, S5_UNITS * S5_UNIT_STATE), lambda d, i: (d, 0, 0)),
        ],
        out_specs=pl.BlockSpec((None, tt, bsz * S5_WIDTH), lambda d, i: (d, tile(d, i), 0)),
        out_shape=jax.ShapeDtypeStruct((2, seq, bsz * S5_WIDTH), BF16),
        scratch_shapes=[pltpu.VMEM((tt * bsz, ncol), F32), pltpu.VMEM((bsz, ncol), F32),
                        pltpu.VMEM((S5_UNITS, tt * bsz, LANES), F32)],
        compiler_params=_cparams("arbitrary", "arbitrary"),
        name="s5_scan",
    )(u_tm, bu, cu, a)


def _s5_params(lam_re, lam_im, log_dt, b_re, b_im, c_re, c_im):
    f = lambda t: t.astype(F32)
    lam_re, lam_im, b_re, b_im, c_re, c_im = map(f, (lam_re, lam_im, b_re, b_im, c_re, c_im))
    dt = jnp.exp(f(log_dt))[:, :, None]
    mag = jnp.exp(lam_re * dt)
    a_re = mag * jnp.cos(lam_im * dt)
    a_im = mag * jnp.sin(lam_im * dt)
    den = lam_re * lam_re + lam_im * lam_im
    f_re = ((a_re - 1.0) * lam_re + a_im * lam_im) / den
    f_im = (a_im * lam_re - (a_re - 1.0) * lam_im) / den
    bb_re = f_re[..., None] * b_re - f_im[..., None] * b_im
    bb_im = f_re[..., None] * b_im + f_im[..., None] * b_re
    ug, nu = S5_UNIT_GROUPS, S5_UNITS
    eye = jnp.eye(ug, dtype=F32)

    def pack_b(t):
        t = t.reshape(2, nu, ug, S5_STATE, S5_GROUP_SIZE)
        t = jnp.einsum('dugnp,gh->dugphn', t, eye)
        return t.reshape(2, nu, LANES, S5_UNIT_STATE)

    def pack_c(t):
        t = t.reshape(2, nu, ug, S5_GROUP_SIZE, S5_STATE)
        t = jnp.einsum('dugpn,gh->duhngp', t, eye)
        return t.reshape(2, nu, S5_UNIT_STATE, LANES)

    bu = jnp.concatenate([pack_b(bb_re), pack_b(bb_im)], axis=-1).astype(BF16)
    cu = jnp.concatenate([pack_c(c_re), -pack_c(c_im)], axis=-2).astype(BF16)
    a = jnp.stack([a_re.reshape(2, -1), a_im.reshape(2, -1)], axis=1)
    return bu, cu, a


POOL_PAD = 16


def _pool_kernel(u_ref, w_ref, s_ref, o_ref, xp_ref, *, seq, rows):
    pad = POOL_PAD
    zeros = jnp.zeros((pad, POOL_WIDTH), F32)
    xp_ref[0:pad, :] = zeros
    xp_ref[pad + seq:2 * pad + seq, :] = zeros
    xp_ref[pad:pad + seq, :] = u_ref[...]
    n = rows + 2 * pad

    def chunk(c, carry):
        r0 = pl.multiple_of(c * rows, rows)
        t = r0 + lax.broadcasted_iota(jnp.int32, (rows, 1), 0)
        for g, win in enumerate(POOL_WINDOWS):
            lanes = slice(POOL_GROUP_SIZE * g, POOL_GROUP_SIZE * (g + 1))
            x = xp_ref[pl.ds(r0, n), lanes]
            s = x
            k = 1
            while k < win:
                s = s + pltpu.roll(s, k, axis=0)
                k *= 2
            left = win // 2
            right = win - 1 - left
            if right:
                s = pltpu.roll(s, n - right, axis=0)
            hi = jnp.minimum(t + right + 1, seq)
            lo = jnp.maximum(t - left, 0)
            mean = s[pad:pad + rows] / (hi - lo).astype(F32)
            pooled = mean - x[pad:pad + rows]
            mixed = jnp.dot(pooled.astype(BF16), w_ref[g], preferred_element_type=F32)
            o_ref[pl.ds(r0, rows), lanes] = (mixed * s_ref[:, lanes]).astype(BF16)
        return carry

    lax.fori_loop(0, seq // rows, chunk, 0)


def _pool(u_pool, w_bf16, scale, bsz, seq, rows=256):
    return pl.pallas_call(
        functools.partial(_pool_kernel, seq=seq, rows=rows),
        grid=(bsz,),
        in_specs=[
            pl.BlockSpec((seq, POOL_WIDTH), lambda b: (b, 0)),
            pl.BlockSpec((len(POOL_WINDOWS), POOL_GROUP_SIZE, POOL_GROUP_SIZE), lambda b: (0, 0, 0)),
            pl.BlockSpec((1, POOL_WIDTH), lambda b: (0, 0)),
        ],
        out_specs=pl.BlockSpec((seq, POOL_WIDTH), lambda b: (b, 0)),
        out_shape=jax.ShapeDtypeStruct((bsz * seq, POOL_WIDTH), BF16),
        scratch_shapes=[pltpu.VMEM((seq + 2 * POOL_PAD, POOL_WIDTH), F32)],
        compiler_params=_cparams("parallel"),
        name="pool",
    )(u_pool, w_bf16, scale)


def _stat_lane(head):
    return head + HEAD_DIM * (1 - head % 2)


def _attn_kernel(q_ref, k_ref, v_ref, bias_ref, o_ref, mx_ref, den_ref, v0_ref, v1_ref, *scr,
                 dil, m_rows, pairs, unroll):
    hc = pl.program_id(1)
    nqb = m_rows // QBLK
    stat_mx, stat_den = (mx_ref, den_ref) if dil == 1 else scr[1:3]
    stat_mx[...] = jnp.zeros_like(stat_mx)
    stat_den[...] = jnp.ones_like(stat_den)

    low_v = lax.broadcasted_iota(jnp.int32, (m_rows, pairs * LANES), 1) % LANES < HEAD_DIM
    for r in range(dil):
        v = v_ref[r]
        v0_ref[r] = jnp.where(low_v, v, jnp.ones_like(v))
        v1_ref[r] = jnp.where(low_v, jnp.ones_like(v), v)

    low = lax.broadcasted_iota(jnp.int32, (QBLK, LANES), 1) < HEAD_DIM

    def qblock(it, carry):
        r = it // nqb
        m0 = pl.multiple_of((it % nqb) * QBLK, QBLK)
        ks = pl.multiple_of(jnp.clip(m0 - QBLK // 2, 0, m_rows - KWIN), QBLK // 2)
        var = (m0 - ks) // (QBLK // 2)
        qrows = pl.ds(m0, QBLK)
        krows = pl.ds(ks, KWIN)
        srows = pl.ds(pl.multiple_of(r * m_rows + m0, QBLK), QBLK)
        for j in range(pairs):
            lanes = slice(LANES * j, LANES * (j + 1))
            qp = q_ref[r, qrows, lanes]
            zero = jnp.zeros_like(qp)
            q2 = jnp.concatenate([jnp.where(low, qp, zero), jnp.where(low, zero, qp)], axis=0)
            s2 = lax.dot_general(q2, k_ref[r, krows, lanes], (((1,), (1,)), ((), ())),
                                 preferred_element_type=F32)
            halves = []
            for e, vsel_ref in enumerate((v0_ref, v1_ref)):
                s = s2[QBLK * e:QBLK * (e + 1)] + bias_ref[var, 2 * (hc * pairs + j) + e]
                mx = jnp.max(s, axis=-1, keepdims=True)
                p = jnp.exp2(s - mx).astype(BF16)
                pv = jnp.dot(p, vsel_ref[r, krows, lanes], preferred_element_type=F32)
                halves.append(pv)
                sl = _stat_lane(2 * j + e)
                stat_mx[srows, sl:sl + 1] = mx
                stat_den[srows, sl:sl + 1] = pv[:, sl:sl + 1]
            o_pair = jnp.where(low, halves[0], halves[1])
            if dil == 1:
                o_ref[qrows, lanes] = o_pair.astype(BF16)
            else:
                scr[0][j, pl.ds(r + dil * m0, QBLK, stride=dil), :] = o_pair
        return carry

    lax.fori_loop(0, dil * nqb, qblock, 0, unroll=unroll)
    if dil > 1:
        for j in range(pairs):
            o_ref[:, LANES * j:LANES * (j + 1)] = scr[0][j].astype(BF16)
        for r in range(dil):
            rows = slice(r * m_rows, (r + 1) * m_rows)
            mx_ref[pl.ds(r, m_rows, stride=dil), :] = stat_mx[rows, :]
            den_ref[pl.ds(r, m_rows, stride=dil), :] = stat_den[rows, :]


def _attn_bias(dil):
    slopes = jnp.exp2(-8.0 * jnp.arange(1, ATTN_HEADS + 1, dtype=F32) / ATTN_HEADS)
    a = jnp.arange(QBLK)[:, None]
    c = jnp.arange(KWIN)[None, :]
    off = (jnp.arange(3) * (QBLK // 2))[:, None, None]
    rel = c[None] - off - a[None]
    dist = (dil * jnp.abs(rel)).astype(F32)
    bias = -LOG2E * slopes[None, :, None, None] * dist[:, None]
    return jnp.where((jnp.abs(rel) <= QBLK // 2)[:, None], bias, NEG_INF)


def _attn_pattern(q, k, v, dil, bsz, seq, hw=ATTN_CHUNK, unroll=8):
    m_rows = seq // dil
    nhc = ATTN_WIDTH // hw
    pairs = hw // LANES
    blk = pl.BlockSpec((None, dil, m_rows, hw), lambda b, h: (b, 0, 0, h))
    stat = pl.BlockSpec((seq, LANES), lambda b, h: (b, h))
    bias = _attn_bias(dil)
    vsel = pltpu.VMEM((dil, m_rows, hw), BF16)
    regroup = [pltpu.VMEM((pairs, seq, LANES), F32), pltpu.VMEM((seq, LANES), F32),
               pltpu.VMEM((seq, LANES), F32)]
    return pl.pallas_call(
        functools.partial(_attn_kernel, dil=dil, m_rows=m_rows, pairs=pairs, unroll=unroll),
        grid=(bsz, nhc),
        in_specs=[blk, blk, blk,
                  pl.BlockSpec(bias.shape, lambda b, h: (0, 0, 0, 0), pipeline_mode=pl.Buffered(1))],
        out_specs=[pl.BlockSpec((seq, hw), lambda b, h: (b, h)), stat, stat],
        out_shape=[jax.ShapeDtypeStruct((bsz * seq, ATTN_WIDTH), BF16),
                   jax.ShapeDtypeStruct((bsz * seq, nhc * LANES), F32),
                   jax.ShapeDtypeStruct((bsz * seq, nhc * LANES), F32)],
        scratch_shapes=[vsel, vsel] + (regroup if dil > 1 else []),
        compiler_params=_cparams("parallel", "parallel"),
        name=f"attn_d{dil}",
    )(q, k, v, bias)


def _merge_kernel(yf_ref, yb_ref, us5_ref, dsk_ref, wglu_ref, ypool_ref,
                  o1_ref, o2_ref, o3_ref, m1_ref, m2_ref, m3_ref, d1_ref, d2_ref, d3_ref, exp_ref,
                  gate_ref, h_ref, wbs_ref, wbp_ref, wba_ref, wout_ref, out_ref):
    dot = functools.partial(jnp.dot, preferred_element_type=F32)
    y = (yf_ref[...].astype(F32) + yb_ref[...].astype(F32)
         + dsk_ref[...] * us5_ref[...].astype(F32))
    y = jax.nn.gelu(y)
    y = y * jax.nn.sigmoid(dot(y.astype(BF16), wglu_ref[...]))
    br_s5 = dot(y.astype(BF16), wbs_ref[...])
    br_pool = dot(ypool_ref[...], wbp_ref[...])

    m1, m2, m3 = m1_ref[...], m2_ref[...], m3_ref[...]
    mx = jnp.maximum(jnp.maximum(m1, m2), m3)
    w1, w2, w3 = jnp.exp2(m1 - mx), jnp.exp2(m2 - mx), jnp.exp2(m3 - mx)
    inv = 1.0 / (w1 * d1_ref[...] + w2 * d2_ref[...] + w3 * d3_ref[...])
    att = []
    for c in range(ATTN_NCHUNK):
        st, ch = slice(LANES * c, LANES * (c + 1)), slice(ATTN_CHUNK * c, ATTN_CHUNK * (c + 1))
        expand = lambda w: dot((w[:, st] * inv[:, st]).astype(BF16), exp_ref[...])
        att.append(expand(w1) * o1_ref[:, ch].astype(F32) + expand(w2) * o2_ref[:, ch].astype(F32)
                   + expand(w3) * o3_ref[:, ch].astype(F32))
    br_attn = dot(jnp.concatenate(att, axis=1).astype(BF16), wba_ref[...])

    merged = (gate_ref[:, 0:D_MODEL].astype(F32) * br_s5
              + gate_ref[:, D_MODEL:2 * D_MODEL].astype(F32) * br_pool
              + gate_ref[:, 2 * D_MODEL:3 * D_MODEL].astype(F32) * br_attn)
    out_ref[...] = h_ref[...] + dot(merged.astype(BF16), wout_ref[...])


def _merge(y_dirs, us5_tm, dsk, wglu, ypool, outs, maxes, dens, gates, h2d, wbs, wbp, wba, wout,
           bsz, seq, tm=512):
    T = bsz * seq
    tps = seq // tm
    row = lambda w: pl.BlockSpec((tm, w), lambda b, i: (b * tps + i, 0))
    full = lambda a: pl.BlockSpec(a.shape, lambda b, i: (0,) * a.ndim)
    ydir = lambda d: pl.BlockSpec((None, tm, S5_WIDTH), lambda b, i: (d, i, b))
    local_head = jnp.arange(ATTN_CHUNK)[None, :] // HEAD_DIM
    expand = (jnp.arange(LANES)[:, None] == _stat_lane(local_head)).astype(BF16)
    return pl.pallas_call(
        _merge_kernel,
        grid=(bsz, tps),
        in_specs=[ydir(0), ydir(1),
                  pl.BlockSpec((tm, S5_WIDTH), lambda b, i: (i, b)),
                  full(dsk), full(wglu), row(POOL_WIDTH),
                  row(ATTN_WIDTH), row(ATTN_WIDTH), row(ATTN_WIDTH),
                  *([row(ATTN_NCHUNK * LANES)] * 6), full(expand),
                  row(3 * D_MODEL), row(D_MODEL),
                  full(wbs), full(wbp), full(wba), full(wout)],
        out_specs=row(D_MODEL),
        out_shape=jax.ShapeDtypeStruct((T, D_MODEL), F32),
        compiler_params=_cparams("parallel", "parallel"),
        name="merge",
    )(y_dirs, y_dirs, us5_tm, dsk, wglu, ypool, *outs, *maxes, *dens, expand, gates, h2d,
      wbs, wbp, wba, wout)


def _mlp_kernel(h_ref, g_ref, wup_ref, wdn_ref, gf_ref, o_ref, *, final, chunk):
    h = h_ref[...]
    hn = _rms(h, g_ref[...]).astype(BF16)
    acc = h
    for c in range(D_FF // chunk):
        cols = slice(chunk * c, chunk * (c + 1))
        up = jnp.dot(hn, wup_ref[:, cols], preferred_element_type=F32)
        act = jnp.square(jnp.maximum(up, 0.0)).astype(BF16)
        acc = acc + jnp.dot(act, wdn_ref[cols, :], preferred_element_type=F32)
    if final:
        acc = _rms(acc, gf_ref[...])
    o_ref[...] = acc


def _mlp(h2d, g, wup, wdn, gf, final, tm=512, chunk=1024):
    T = h2d.shape[0]
    const = lambda a: pl.BlockSpec(a.shape, lambda m: (0,) * a.ndim, pipeline_mode=pl.Buffered(1))
    return pl.pallas_call(
        functools.partial(_mlp_kernel, final=final, chunk=chunk),
        grid=(T // tm,),
        in_specs=[pl.BlockSpec((tm, D_MODEL), lambda m: (m, 0)),
                  const(g), const(wup), const(wdn), const(gf)],
        out_specs=pl.BlockSpec((tm, D_MODEL), lambda m: (m, 0)),
        out_shape=jax.ShapeDtypeStruct((T, D_MODEL), F32),
        compiler_params=_cparams("parallel"),
        name="mlp",
    )(h2d, g, wup, wdn, gf)


def kernel(x, norm_mix, w_in, s5_lam_re, s5_lam_im, s5_log_dt, s5_b_re, s5_b_im, s5_c_re, s5_c_im, s5_d, s5_w_glu, pool_w, pool_scale, w_branch_s5, w_branch_pool, w_branch_attn, w_out, norm_mlp, w_up, w_down, norm_final):
    bsz, seq, _ = x.shape
    depth = w_in.shape[0]
    bf = lambda t: t.astype(BF16)
    row = lambda t: t.astype(F32).reshape(1, -1)
    h = x.astype(F32).reshape(bsz * seq, D_MODEL)
    for l in range(depth):
        us5_tm, u_pool, q, k, v, gates = _inproj(h, row(norm_mix[l]), bf(w_in[l]), bsz, seq)
        bu, cu, a = _s5_params(s5_lam_re[l], s5_lam_im[l], s5_log_dt[l], s5_b_re[l], s5_b_im[l],
                               s5_c_re[l], s5_c_im[l])
        y_dirs = _s5_scan(us5_tm, bu, cu, a, bsz, seq)
        y_pool = _pool(u_pool, bf(pool_w[l]), row(pool_scale[l]), bsz, seq)
        outs, maxes, dens = zip(*[_attn_pattern(q[i], k[i], v[i], dil, bsz, seq)
                                  for i, dil in enumerate(DILATIONS)])
        h = _merge(y_dirs, us5_tm, row(s5_d[l]), bf(s5_w_glu[l]), y_pool, outs, maxes, dens, gates, h,
                   bf(w_branch_s5[l]), bf(w_branch_pool[l]), bf(w_branch_attn[l]), bf(w_out[l]),
                   bsz, seq)
        h = _mlp(h, row(norm_mlp[l]), bf(w_up[l]), bf(w_down[l]), row(norm_final), l == depth - 1)
    return h.reshape(bsz, seq, D_MODEL).astype(x.dtype)
```

```python
import functools
import math

import jax
import jax.numpy as jnp
from jax import lax
from jax.experimental import pallas as pl
from jax.experimental.pallas import tpu as pltpu

F32 = jnp.float32
BF16 = jnp.bfloat16

D_MODEL = 1024
S5_WIDTH = 512
S5_GROUP_SIZE = 16
S5_STATE = 64
POOL_WIDTH = 512
POOL_WINDOWS = (2, 4, 8, 16)
POOL_GROUP_SIZE = 128
ATTN_HEADS = 16
HEAD_DIM = 64
ATTN_WIDTH = 1024
ATTN_PATTERNS = ((128, 1), (512, 4), (2048, 16))
D_FF = 4096
NORM_EPS = 1e-6
NEG_INF = -1e30

LANES = 128
SUBLANES = 8
S5_UNIT_GROUPS = LANES // S5_GROUP_SIZE
S5_UNITS = S5_WIDTH // LANES
S5_UNIT_STATE = S5_UNIT_GROUPS * S5_STATE
QBLK = 128
KWIN = 2 * QBLK
ATTN_CHUNK = 256
ATTN_NCHUNK = ATTN_WIDTH // ATTN_CHUNK
VMEM_LIMIT = 56 * 1024 * 1024


def _cparams(*sem):
    return pltpu.CompilerParams(dimension_semantics=sem, vmem_limit_bytes=VMEM_LIMIT)


def _rms(x, g):
    ms = jnp.mean(x * x, axis=-1, keepdims=True)
    return x * lax.rsqrt(ms + NORM_EPS) * g


DILATIONS = tuple(dil for _, dil in ATTN_PATTERNS)
LOG2E = math.log2(math.e)
Q_SCALE = HEAD_DIM ** -0.5 * LOG2E


def _inproj_kernel(x_ref, g_ref, w_ref, us5_ref, upool_ref, *rest, tm):
    nd = len(DILATIONS)
    qkv_refs = [rest[nd * i:nd * (i + 1)] for i in range(3)]
    gate_ref = rest[3 * nd]
    lay_refs = dict(zip(DILATIONS[:-1], rest[3 * nd + 1:]))
    xn = _rms(x_ref[...], g_ref[...]).astype(BF16)
    proj = lambda c: jnp.dot(xn, w_ref[:, D_MODEL * c:D_MODEL * (c + 1)], preferred_element_type=F32)

    acc = proj(0)
    us5_ref[...] = acc[:, :S5_WIDTH].astype(BF16)
    upool_ref[...] = acc[:, S5_WIDTH:]

    for i, refs in enumerate(qkv_refs):
        acc = proj(1 + i)
        if i == 0:
            acc = acc * Q_SCALE
        refs[0][0] = acc.astype(BF16)
        for j in range(ATTN_WIDTH // LANES):
            lay_refs[1][j] = acc[:, LANES * j:LANES * (j + 1)]
        for base, dil, ref in zip(DILATIONS[:-1], DILATIONS[1:], refs[1:]):
            ratio, rows = dil // base, tm // dil
            for c in range(base):
                for a in range(ratio):
                    r = c + base * a
                    for j in range(ATTN_WIDTH // LANES):
                        piece = lay_refs[base][j, pl.ds(c * (tm // base) + a, rows, stride=ratio), :]
                        ref[r, :, LANES * j:LANES * (j + 1)] = piece.astype(BF16)
                        if dil in lay_refs:
                            lay_refs[dil][j, r * rows:(r + 1) * rows, :] = piece

    for c in range(3):
        gate_ref[:, D_MODEL * c:D_MODEL * (c + 1)] = jax.nn.sigmoid(proj(4 + c)).astype(BF16)


def _inproj(h2d, g, w_bf16, bsz, seq, tm=512):
    T = bsz * seq
    tps = seq // tm
    row = lambda m: (m, 0)
    const = lambda a: pl.BlockSpec(a.shape, lambda m: (0,) * a.ndim, pipeline_mode=pl.Buffered(1))
    lay_specs = [pl.BlockSpec((None, dil, tm // dil, ATTN_WIDTH), lambda m: (m // tps, 0, m % tps, 0))
                 for dil in DILATIONS]
    lay_shapes = [jax.ShapeDtypeStruct((bsz, dil, seq // dil, ATTN_WIDTH), BF16) for dil in DILATIONS]
    nd = len(DILATIONS)
    outs = pl.pallas_call(
        functools.partial(_inproj_kernel, tm=tm),
        grid=(T // tm,),
        in_specs=[pl.BlockSpec((tm, D_MODEL), row), const(g), const(w_bf16)],
        out_specs=[
            pl.BlockSpec((tm, S5_WIDTH), lambda m: (m % tps, m // tps)),
            pl.BlockSpec((tm, POOL_WIDTH), row),
            *(lay_specs * 3),
            pl.BlockSpec((tm, 3 * D_MODEL), row),
        ],
        out_shape=[
            jax.ShapeDtypeStruct((seq, bsz * S5_WIDTH), BF16),
            jax.ShapeDtypeStruct((T, POOL_WIDTH), F32),
            *(lay_shapes * 3),
            jax.ShapeDtypeStruct((T, 3 * D_MODEL), BF16),
        ],
        scratch_shapes=[pltpu.VMEM((ATTN_WIDTH // LANES, tm, LANES), F32)] * (nd - 1),
        compiler_params=_cparams("parallel"),
        name="inproj",
    )(h2d, g, w_bf16)
    q, k, v = (outs[2 + nd * i:2 + nd * (i + 1)] for i in range(3))
    return outs[0], outs[1], q, k, v, outs[2 + 3 * nd]


def _s5_kernel(u_ref, bu_ref, cu_ref, cb_ref, a_ref, y_ref, st_ref, tmp_ref, xe_ref, xo_ref,
               p_ref, ypar_ref, ynat_ref, *sig_refs, tt, bsz):
    d = pl.program_id(0)
    i = pl.program_id(1)
    hs = S5_UNIT_STATE
    npair = tt // 2
    prows = npair * bsz
    nu = S5_UNITS

    @pl.when(i == 0)
    def _():
        st_ref[...] = jnp.zeros_like(st_ref)

    for b in range(bsz):
        for j in range(nu):
            lanes = slice(S5_WIDTH * b + LANES * j, S5_WIDTH * b + LANES * (j + 1))
            tmp_ref[b * nu + j] = u_ref[:, lanes].astype(F32)
            xe_ref[j, pl.ds(b, npair, stride=bsz), :] = tmp_ref[b * nu + j, pl.ds(0, npair, stride=2), :]
            xo_ref[j, pl.ds(b, npair, stride=bsz), :] = tmp_ref[b * nu + j, pl.ds(1, npair, stride=2), :]

    lead = pl.multiple_of((1 - d) * bsz, bsz)
    carry_rows = pl.ds(pl.multiple_of(d * prows, bsz), bsz)
    x2 = []
    for j in range(nu):
        x2.append(jnp.concatenate([xe_ref[j], xo_ref[j]], axis=1).astype(BF16))
        sig_refs[j][pl.ds(lead, prows), :] = jnp.dot(x2[j], bu_ref[j], preferred_element_type=F32)
        sig_refs[j][carry_rows, :] = st_ref[:, 2 * hs * j:2 * hs * (j + 1)]

    for j0 in range(0, nu, 2):
        units = (j0, j0 + 1)
        coef = [(jnp.broadcast_to(a_ref[0:1, hs * j:hs * (j + 1)], (bsz, hs)),
                 jnp.broadcast_to(a_ref[1:2, hs * j:hs * (j + 1)], (bsz, hs))) for j in units]
        init = tuple((st_ref[:, 2 * hs * j:2 * hs * j + hs], st_ref[:, 2 * hs * j + hs:2 * hs * (j + 1)])
                     for j in units)

        def step(s, carry, units=units, coef=coef):
            m = jnp.where(d == 0, s, npair - 1 - s)
            rows = pl.ds(pl.multiple_of(m * bsz + lead, bsz), bsz)
            out = []
            for j, (ar, ai), (sr, si) in zip(units, coef, carry):
                sig = sig_refs[j]
                nr = ar * sr - ai * si + sig[rows, :hs]
                ni = ar * si + ai * sr + sig[rows, hs:]
                sig[rows, :hs] = nr
                sig[rows, hs:] = ni
                out.append((nr, ni))
            return tuple(out)

        final = lax.fori_loop(0, npair, step, init, unroll=2)
        for j, (sr, si) in zip(units, final):
            st_ref[:, 2 * hs * j:2 * hs * j + hs] = sr
            st_ref[:, 2 * hs * j + hs:2 * hs * (j + 1)] = si

    for j in range(nu):
        p_ref[j] = jnp.dot(sig_refs[j][...].astype(BF16), cu_ref[j], preferred_element_type=F32)
        ypar_ref[0, j] = p_ref[j, pl.ds(lead, prows), :LANES]
        ypar_ref[1, j] = (p_ref[j, pl.ds(bsz - lead, prows), LANES:]
                          + jnp.dot(x2[j], cb_ref[j], preferred_element_type=F32))

    for b in range(bsz):
        for j in range(nu):
            lanes = slice(S5_WIDTH * b + LANES * j, S5_WIDTH * b + LANES * (j + 1))
            k = b * nu + j
            ynat_ref[k, pl.ds(1 - d, npair, stride=2), :] = ypar_ref[0, j, pl.ds(b, npair, stride=bsz), :]
            ynat_ref[k, pl.ds(d, npair, stride=2), :] = ypar_ref[1, j, pl.ds(b, npair, stride=bsz), :]
            y_ref[:, lanes] = ynat_ref[k].astype(BF16)


def _s5_scan(u_tm, bu, cu, cb, a2, bsz, seq, tt=128):
    assert bsz == SUBLANES and tt % 2 == 0
    n_t = seq // tt
    tile = lambda d, i: i + d * (n_t - 1 - 2 * i)
    uw = 2 * S5_UNIT_STATE
    prows = tt // 2 * bsz
    per_dir = lambda a: pl.BlockSpec((None,) + a.shape[1:], lambda d, i: (d,) + (0,) * (a.ndim - 1))
    return pl.pallas_call(
        functools.partial(_s5_kernel, tt=tt, bsz=bsz),
        grid=(2, n_t),
        in_specs=[pl.BlockSpec((tt, bsz * S5_WIDTH), lambda d, i: (tile(d, i), 0)),
                  per_dir(bu), per_dir(cu), per_dir(cb), per_dir(a2)],
        out_specs=pl.BlockSpec((None, tt, bsz * S5_WIDTH), lambda d, i: (d, tile(d, i), 0)),
        out_shape=jax.ShapeDtypeStruct((2, seq, bsz * S5_WIDTH), BF16),
        scratch_shapes=[pltpu.VMEM((bsz, S5_UNITS * uw), F32),
                        pltpu.VMEM((bsz * S5_UNITS, tt, LANES), F32),
                        pltpu.VMEM((S5_UNITS, prows, LANES), F32),
                        pltpu.VMEM((S5_UNITS, prows, LANES), F32),
                        pltpu.VMEM((S5_UNITS, prows + bsz, 2 * LANES), F32),
                        pltpu.VMEM((2, S5_UNITS, prows, LANES), F32),
                        pltpu.VMEM((bsz * S5_UNITS, tt, LANES), F32)]
        + [pltpu.VMEM((prows + bsz, uw), F32)] * S5_UNITS,
        compiler_params=_cparams("arbitrary", "arbitrary"),
        name="s5_scan",
    )(u_tm, bu, cu, cb, a2)


def _s5_params(lam_re, lam_im, log_dt, b_re, b_im, c_re, c_im):
    f = lambda t: t.astype(F32)
    lam_re, lam_im, b_re, b_im, c_re, c_im = map(f, (lam_re, lam_im, b_re, b_im, c_re, c_im))
    dt = jnp.exp(f(log_dt))[:, :, None]
    mag = jnp.exp(lam_re * dt)
    a_re = mag * jnp.cos(lam_im * dt)
    a_im = mag * jnp.sin(lam_im * dt)
    den = lam_re * lam_re + lam_im * lam_im
    f_re = ((a_re - 1.0) * lam_re + a_im * lam_im) / den
    f_im = (a_im * lam_re - (a_re - 1.0) * lam_im) / den
    bb_re = f_re[..., None] * b_re - f_im[..., None] * b_im
    bb_im = f_re[..., None] * b_im + f_im[..., None] * b_re
    ug, nu = S5_UNIT_GROUPS, S5_UNITS
    eye = jnp.eye(ug, dtype=F32)

    def pack_b(t):
        t = t.reshape(2, nu, ug, S5_STATE, S5_GROUP_SIZE)
        t = jnp.einsum('dugnp,gh->dugphn', t, eye)
        return t.reshape(2, nu, LANES, S5_UNIT_STATE)

    def pack_c(t):
        t = t.reshape(2, nu, ug, S5_GROUP_SIZE, S5_STATE)
        t = jnp.einsum('dugpn,gh->duhngp', t, eye)
        return t.reshape(2, nu, S5_UNIT_STATE, LANES)

    def pack_pp(t):
        t = t.reshape(2, nu, ug, S5_GROUP_SIZE, S5_GROUP_SIZE)
        t = jnp.einsum('dugpq,gh->dugqhp', t, eye)
        return t.reshape(2, nu, LANES, LANES)

    ab_re = a_re[..., None] * bb_re - a_im[..., None] * bb_im
    ab_im = a_re[..., None] * bb_im + a_im[..., None] * bb_re
    b_plain = jnp.concatenate([pack_b(bb_re), pack_b(bb_im)], axis=-1)
    b_first = jnp.concatenate([pack_b(ab_re), pack_b(ab_im)], axis=-1)
    bu = jnp.stack([jnp.concatenate([b_first[0], b_plain[0]], axis=-2),
                    jnp.concatenate([b_plain[1], b_first[1]], axis=-2)]).astype(BF16)

    ca_re = c_re * a_re[:, :, None, :] - c_im * a_im[:, :, None, :]
    ca_im = c_re * a_im[:, :, None, :] + c_im * a_re[:, :, None, :]
    c_plain = jnp.concatenate([pack_c(c_re), -pack_c(c_im)], axis=-2)
    c_a = jnp.concatenate([pack_c(ca_re), -pack_c(ca_im)], axis=-2)
    cu = jnp.concatenate([c_plain, c_a], axis=-1).astype(BF16)

    cb_pq = (jnp.einsum('dgpn,dgnq->dgpq', c_re, bb_re) - jnp.einsum('dgpn,dgnq->dgpq', c_im, bb_im))
    cb_mat = pack_pp(cb_pq)
    zero = jnp.zeros_like(cb_mat[0])
    cb = jnp.stack([jnp.concatenate([cb_mat[0], zero], axis=-2),
                    jnp.concatenate([zero, cb_mat[1]], axis=-2)]).astype(BF16)

    a2_re = a_re * a_re - a_im * a_im
    a2_im = 2.0 * a_re * a_im
    a2 = jnp.stack([a2_re.reshape(2, -1), a2_im.reshape(2, -1)], axis=1)
    return bu, cu, cb, a2


POOL_PAD = 16


def _pool_kernel(u_ref, w_ref, s_ref, o_ref, xp_ref, *, seq, rows):
    pad = POOL_PAD
    zeros = jnp.zeros((pad, POOL_WIDTH), F32)
    xp_ref[0:pad, :] = zeros
    xp_ref[pad + seq:2 * pad + seq, :] = zeros
    xp_ref[pad:pad + seq, :] = u_ref[...]
    n = rows + 2 * pad

    def chunk(c, carry):
        r0 = pl.multiple_of(c * rows, rows)
        t = r0 + lax.broadcasted_iota(jnp.int32, (rows, 1), 0)
        for g, win in enumerate(POOL_WINDOWS):
            lanes = slice(POOL_GROUP_SIZE * g, POOL_GROUP_SIZE * (g + 1))
            x = xp_ref[pl.ds(r0, n), lanes]
            s = x
            k = 1
            while k < win:
                s = s + pltpu.roll(s, k, axis=0)
                k *= 2
            left = win // 2
            right = win - 1 - left
            if right:
                s = pltpu.roll(s, n - right, axis=0)
            hi = jnp.minimum(t + right + 1, seq)
            lo = jnp.maximum(t - left, 0)
            mean = s[pad:pad + rows] / (hi - lo).astype(F32)
            pooled = mean - x[pad:pad + rows]
            mixed = jnp.dot(pooled.astype(BF16), w_ref[g], preferred_element_type=F32)
            o_ref[pl.ds(r0, rows), lanes] = (mixed * s_ref[:, lanes]).astype(BF16)
        return carry

    lax.fori_loop(0, seq // rows, chunk, 0)


def _pool(u_pool, w_bf16, scale, bsz, seq, rows=256):
    return pl.pallas_call(
        functools.partial(_pool_kernel, seq=seq, rows=rows),
        grid=(bsz,),
        in_specs=[
            pl.BlockSpec((seq, POOL_WIDTH), lambda b: (b, 0)),
            pl.BlockSpec((len(POOL_WINDOWS), POOL_GROUP_SIZE, POOL_GROUP_SIZE), lambda b: (0, 0, 0)),
            pl.BlockSpec((1, POOL_WIDTH), lambda b: (0, 0)),
        ],
        out_specs=pl.BlockSpec((seq, POOL_WIDTH), lambda b: (b, 0)),
        out_shape=jax.ShapeDtypeStruct((bsz * seq, POOL_WIDTH), BF16),
        scratch_shapes=[pltpu.VMEM((seq + 2 * POOL_PAD, POOL_WIDTH), F32)],
        compiler_params=_cparams("parallel"),
        name="pool",
    )(u_pool, w_bf16, scale)


def _stat_lane(head):
    return head + HEAD_DIM * (1 - head % 2)


STAT_MAX_SHIFT = 8


def _attn_kernel(q_ref, k_ref, v_ref, bias_ref, o_ref, stat_ref, v0_ref, v1_ref, *scr,
                 dil, m_rows, pairs, unroll):
    hc = pl.program_id(1)
    nqb = m_rows // QBLK
    stat = stat_ref if dil == 1 else scr[1]
    stat[...] = jnp.ones_like(stat)

    low_v = lax.broadcasted_iota(jnp.int32, (m_rows, pairs * LANES), 1) % LANES < HEAD_DIM
    for r in range(dil):
        v = v_ref[r]
        v0_ref[r] = jnp.where(low_v, v, jnp.ones_like(v))
        v1_ref[r] = jnp.where(low_v, jnp.ones_like(v), v)

    low = lax.broadcasted_iota(jnp.int32, (QBLK, LANES), 1) < HEAD_DIM

    def qblock(it, carry):
        r = it // nqb
        m0 = pl.multiple_of((it % nqb) * QBLK, QBLK)
        ks = pl.multiple_of(jnp.clip(m0 - QBLK // 2, 0, m_rows - KWIN), QBLK // 2)
        var = (m0 - ks) // (QBLK // 2)
        qrows = pl.ds(m0, QBLK)
        krows = pl.ds(ks, KWIN)
        srows = pl.ds(pl.multiple_of(r * m_rows + m0, QBLK), QBLK)
        for j in range(pairs):
            lanes = slice(LANES * j, LANES * (j + 1))
            qp = q_ref[r, qrows, lanes]
            zero = jnp.zeros_like(qp)
            q2 = jnp.concatenate([jnp.where(low, qp, zero), jnp.where(low, zero, qp)], axis=0)
            s2 = lax.dot_general(q2, k_ref[r, krows, lanes], (((1,), (1,)), ((), ())),
                                 preferred_element_type=F32)
            halves = []
            for e, vsel_ref in enumerate((v0_ref, v1_ref)):
                s = s2[QBLK * e:QBLK * (e + 1)] + bias_ref[var, 2 * (hc * pairs + j) + e]
                mx = jnp.max(s, axis=-1, keepdims=True)
                p = jnp.exp2(s - mx).astype(BF16)
                pv = jnp.dot(p, vsel_ref[r, krows, lanes], preferred_element_type=F32)
                halves.append(pv)
                sl = _stat_lane(2 * j + e)
                stat[srows, sl:sl + 1] = pv[:, sl:sl + 1]
                stat[srows, sl + STAT_MAX_SHIFT:sl + STAT_MAX_SHIFT + 1] = mx
            o_pair = jnp.where(low, halves[0], halves[1])
            if dil == 1:
                o_ref[qrows, lanes] = o_pair.astype(BF16)
            else:
                scr[0][j, pl.ds(r + dil * m0, QBLK, stride=dil), :] = o_pair
        return carry

    lax.fori_loop(0, dil * nqb, qblock, 0, unroll=unroll)
    if dil > 1:
        for j in range(pairs):
            o_ref[:, LANES * j:LANES * (j + 1)] = scr[0][j].astype(BF16)
        for r in range(dil):
            rows = slice(r * m_rows, (r + 1) * m_rows)
            stat_ref[pl.ds(r, m_rows, stride=dil), :] = stat[rows, :]


def _attn_bias(dil):
    slopes = jnp.exp2(-8.0 * jnp.arange(1, ATTN_HEADS + 1, dtype=F32) / ATTN_HEADS)
    a = jnp.arange(QBLK)[:, None]
    c = jnp.arange(KWIN)[None, :]
    off = (jnp.arange(3) * (QBLK // 2))[:, None, None]
    rel = c[None] - off - a[None]
    dist = (dil * jnp.abs(rel)).astype(F32)
    bias = -LOG2E * slopes[None, :, None, None] * dist[:, None]
    return jnp.where((jnp.abs(rel) <= QBLK // 2)[:, None], bias, NEG_INF)


def _attn_pattern(q, k, v, dil, bsz, seq, hw=ATTN_CHUNK, unroll=8):
    m_rows = seq // dil
    nhc = ATTN_WIDTH // hw
    pairs = hw // LANES
    blk = pl.BlockSpec((None, dil, m_rows, hw), lambda b, h: (b, 0, 0, h))
    stat = pl.BlockSpec((seq, LANES), lambda b, h: (b, h))
    bias = _attn_bias(dil)
    vsel = pltpu.VMEM((dil, m_rows, hw), BF16)
    regroup = [pltpu.VMEM((pairs, seq, LANES), F32), pltpu.VMEM((seq, LANES), F32)]
    return pl.pallas_call(
        functools.partial(_attn_kernel, dil=dil, m_rows=m_rows, pairs=pairs, unroll=unroll),
        grid=(bsz, nhc),
        in_specs=[blk, blk, blk,
                  pl.BlockSpec(bias.shape, lambda b, h: (0, 0, 0, 0), pipeline_mode=pl.Buffered(1))],
        out_specs=[pl.BlockSpec((seq, hw), lambda b, h: (b, h)), stat],
        out_shape=[jax.ShapeDtypeStruct((bsz * seq, ATTN_WIDTH), BF16),
                   jax.ShapeDtypeStruct((bsz * seq, nhc * LANES), F32)],
        scratch_shapes=[vsel, vsel] + (regroup if dil > 1 else []),
        compiler_params=_cparams("parallel", "parallel"),
        name=f"attn_d{dil}",
    )(q, k, v, bias)


def _merge_kernel(yf_ref, yb_ref, us5_ref, dsk_ref, wglu_ref, ypool_ref,
                  o1_ref, o2_ref, o3_ref, s1_ref, s2_ref, s3_ref, exp_ref,
                  gate_ref, h_ref, wbs_ref, wbp_ref, wba_ref, wout_ref, out_ref):
    dot = functools.partial(jnp.dot, preferred_element_type=F32)
    y = (yf_ref[...].astype(F32) + yb_ref[...].astype(F32)
         + dsk_ref[...] * us5_ref[...].astype(F32))
    y = jax.nn.gelu(y)
    y = y * jax.nn.sigmoid(dot(y.astype(BF16), wglu_ref[...]))
    br_s5 = dot(y.astype(BF16), wbs_ref[...])
    br_pool = dot(ypool_ref[...], wbp_ref[...])

    lane = lax.broadcasted_iota(jnp.int32, (h_ref.shape[0], LANES), 1)
    is_stat = functools.reduce(jnp.logical_or,
                               [lane == _stat_lane(hd) for hd in range(ATTN_CHUNK // HEAD_DIM)])
    att = []
    for c in range(ATTN_NCHUNK):
        st, ch = slice(LANES * c, LANES * (c + 1)), slice(ATTN_CHUNK * c, ATTN_CHUNK * (c + 1))
        d1, d2, d3 = s1_ref[:, st], s2_ref[:, st], s3_ref[:, st]
        m1, m2, m3 = (pltpu.roll(s, LANES - STAT_MAX_SHIFT, axis=1) for s in (d1, d2, d3))
        mx = jnp.maximum(jnp.maximum(m1, m2), m3)
        w1, w2, w3 = jnp.exp2(m1 - mx), jnp.exp2(m2 - mx), jnp.exp2(m3 - mx)
        inv = 1.0 / (w1 * d1 + w2 * d2 + w3 * d3)
        expand = lambda w: dot(jnp.where(is_stat, w * inv, 0.0).astype(BF16), exp_ref[...])
        att.append(expand(w1) * o1_ref[:, ch].astype(F32) + expand(w2) * o2_ref[:, ch].astype(F32)
                   + expand(w3) * o3_ref[:, ch].astype(F32))
    br_attn = dot(jnp.concatenate(att, axis=1).astype(BF16), wba_ref[...])

    merged = (gate_ref[:, 0:D_MODEL].astype(F32) * br_s5
              + gate_ref[:, D_MODEL:2 * D_MODEL].astype(F32) * br_pool
              + gate_ref[:, 2 * D_MODEL:3 * D_MODEL].astype(F32) * br_attn)
    out_ref[...] = h_ref[...] + dot(merged.astype(BF16), wout_ref[...])


def _merge(y_dirs, us5_tm, dsk, wglu, ypool, outs, stats, gates, h2d, wbs, wbp, wba, wout,
           bsz, seq, tm=512):
    T = bsz * seq
    tps = seq // tm
    row = lambda w: pl.BlockSpec((tm, w), lambda b, i: (b * tps + i, 0))
    full = lambda a: pl.BlockSpec(a.shape, lambda b, i: (0,) * a.ndim)
    ydir = lambda d: pl.BlockSpec((None, tm, S5_WIDTH), lambda b, i: (d, i, b))
    local_head = jnp.arange(ATTN_CHUNK)[None, :] // HEAD_DIM
    expand = (jnp.arange(LANES)[:, None] == _stat_lane(local_head)).astype(BF16)
    return pl.pallas_call(
        _merge_kernel,
        grid=(bsz, tps),
        in_specs=[ydir(0), ydir(1),
                  pl.BlockSpec((tm, S5_WIDTH), lambda b, i: (i, b)),
                  full(dsk), full(wglu), row(POOL_WIDTH),
                  row(ATTN_WIDTH), row(ATTN_WIDTH), row(ATTN_WIDTH),
                  *([row(ATTN_NCHUNK * LANES)] * 3), full(expand),
                  row(3 * D_MODEL), row(D_MODEL),
                  full(wbs), full(wbp), full(wba), full(wout)],
        out_specs=row(D_MODEL),
        out_shape=jax.ShapeDtypeStruct((T, D_MODEL), F32),
        compiler_params=_cparams("parallel", "parallel"),
        name="merge",
    )(y_dirs, y_dirs, us5_tm, dsk, wglu, ypool, *outs, *stats, expand, gates, h2d,
      wbs, wbp, wba, wout)


def _mlp_kernel(h_ref, g_ref, wup_ref, wdn_ref, gf_ref, o_ref, *, final, chunk):
    h = h_ref[...]
    hn = _rms(h, g_ref[...]).astype(BF16)
    acc = h
    for c in range(D_FF // chunk):
        cols = slice(chunk * c, chunk * (c + 1))
        up = jnp.dot(hn, wup_ref[:, cols], preferred_element_type=F32)
        act = jnp.square(jnp.maximum(up, 0.0)).astype(BF16)
        acc = acc + jnp.dot(act, wdn_ref[cols, :], preferred_element_type=F32)
    if final:
        acc = _rms(acc, gf_ref[...])
    o_ref[...] = acc


def _mlp(h2d, g, wup, wdn, gf, final, tm=512, chunk=1024):
    T = h2d.shape[0]
    const = lambda a: pl.BlockSpec(a.shape, lambda m: (0,) * a.ndim, pipeline_mode=pl.Buffered(1))
    return pl.pallas_call(
        functools.partial(_mlp_kernel, final=final, chunk=chunk),
        grid=(T // tm,),
        in_specs=[pl.BlockSpec((tm, D_MODEL), lambda m: (m, 0)),
                  const(g), const(wup), const(wdn), const(gf)],
        out_specs=pl.BlockSpec((tm, D_MODEL), lambda m: (m, 0)),
        out_shape=jax.ShapeDtypeStruct((T, D_MODEL), F32),
        compiler_params=_cparams("parallel"),
        name="mlp",
    )(h2d, g, wup, wdn, gf)


def kernel(x, norm_mix, w_in, s5_lam_re, s5_lam_im, s5_log_dt, s5_b_re, s5_b_im, s5_c_re, s5_c_im, s5_d, s5_w_glu, pool_w, pool_scale, w_branch_s5, w_branch_pool, w_branch_attn, w_out, norm_mlp, w_up, w_down, norm_final):
    bsz, seq, _ = x.shape
    depth = w_in.shape[0]
    bf = lambda t: t.astype(BF16)
    row = lambda t: t.astype(F32).reshape(1, -1)
    h = x.astype(F32).reshape(bsz * seq, D_MODEL)
    for l in range(depth):
        us5_tm, u_pool, q, k, v, gates = _inproj(h, row(norm_mix[l]), bf(w_in[l]), bsz, seq)
        s5_w = _s5_params(s5_lam_re[l], s5_lam_im[l], s5_log_dt[l], s5_b_re[l], s5_b_im[l],
                               s5_c_re[l], s5_c_im[l])
        y_dirs = _s5_scan(us5_tm, *s5_w, bsz, seq)
        y_pool = _pool(u_pool, bf(pool_w[l]), row(pool_scale[l]), bsz, seq)
        outs, stats = zip(*[_attn_pattern(q[i], k[i], v[i], dil, bsz, seq)
                            for i, dil in enumerate(DILATIONS)])
        h = _merge(y_dirs, us5_tm, row(s5_d[l]), bf(s5_w_glu[l]), y_pool, outs, stats, gates, h,
                   bf(w_branch_s5[l]), bf(w_branch_pool[l]), bf(w_branch_attn[l]), bf(w_out[l]),
                   bsz, seq)
        h = _mlp(h, row(norm_mlp[l]), bf(w_up[l]), bf(w_down[l]), row(norm_final), l == depth - 1)
    return h.reshape(bsz, seq, D_MODEL).astype(x.dtype)
```

```python
import functools
import math

import jax
import jax.numpy as jnp
from jax import lax
from jax.experimental import pallas as pl
from jax.experimental.pallas import tpu as pltpu

F32 = jnp.float32
BF16 = jnp.bfloat16

D_MODEL = 1024
S5_WIDTH = 512
S5_GROUP_SIZE = 16
S5_STATE = 64
POOL_WIDTH = 512
POOL_WINDOWS = (2, 4, 8, 16)
POOL_GROUP_SIZE = 128
ATTN_HEADS = 16
HEAD_DIM = 64
ATTN_WIDTH = 1024
ATTN_PATTERNS = ((128, 1), (512, 4), (2048, 16))
D_FF = 4096
NORM_EPS = 1e-6
NEG_INF = -1e30

LANES = 128
SUBLANES = 8
S5_UNIT_GROUPS = LANES // S5_GROUP_SIZE
S5_UNITS = S5_WIDTH // LANES
S5_UNIT_STATE = S5_UNIT_GROUPS * S5_STATE
QBLK = 128
KWIN = 2 * QBLK
ATTN_CHUNK = 256
ATTN_NCHUNK = ATTN_WIDTH // ATTN_CHUNK
VMEM_LIMIT = 56 * 1024 * 1024


def _cparams(*sem):
    return pltpu.CompilerParams(dimension_semantics=sem, vmem_limit_bytes=VMEM_LIMIT)


def _rms(x, g):
    ms = jnp.mean(x * x, axis=-1, keepdims=True)
    return x * lax.rsqrt(ms + NORM_EPS) * g


def _sigmoid(x):
    return 0.5 + 0.5 * jnp.tanh(0.5 * x)


DILATIONS = tuple(dil for _, dil in ATTN_PATTERNS)
LOG2E = math.log2(math.e)
Q_SCALE = HEAD_DIM ** -0.5 * LOG2E


def _inproj_kernel(x_ref, g_ref, w_ref, us5_ref, upool_ref, *rest, tm):
    nd = len(DILATIONS)
    qkv_refs = [rest[nd * i:nd * (i + 1)] for i in range(3)]
    gate_ref = rest[3 * nd]
    lay_refs = dict(zip(DILATIONS[:-1], rest[3 * nd + 1:]))
    xn = _rms(x_ref[...], g_ref[...]).astype(BF16)
    proj = lambda c: jnp.dot(xn, w_ref[:, D_MODEL * c:D_MODEL * (c + 1)], preferred_element_type=F32)

    acc = proj(0)
    us5_ref[...] = acc[:, :S5_WIDTH].astype(BF16)
    upool_ref[...] = acc[:, S5_WIDTH:]

    for i, refs in enumerate(qkv_refs):
        acc = proj(1 + i)
        if i == 0:
            acc = acc * Q_SCALE
        refs[0][0] = acc.astype(BF16)
        for j in range(ATTN_WIDTH // LANES):
            lay_refs[1][j] = acc[:, LANES * j:LANES * (j + 1)]
        for base, dil, ref in zip(DILATIONS[:-1], DILATIONS[1:], refs[1:]):
            ratio, rows = dil // base, tm // dil
            for c in range(base):
                for a in range(ratio):
                    r = c + base * a
                    for j in range(ATTN_WIDTH // LANES):
                        piece = lay_refs[base][j, pl.ds(c * (tm // base) + a, rows, stride=ratio), :]
                        ref[r, :, LANES * j:LANES * (j + 1)] = piece.astype(BF16)
                        if dil in lay_refs:
                            lay_refs[dil][j, r * rows:(r + 1) * rows, :] = piece

    for c in range(3):
        gate_ref[:, D_MODEL * c:D_MODEL * (c + 1)] = _sigmoid(proj(4 + c)).astype(BF16)


def _inproj(h2d, g, w_bf16, bsz, seq, tm=512):
    T = bsz * seq
    tps = seq // tm
    row = lambda m: (m, 0)
    const = lambda a: pl.BlockSpec(a.shape, lambda m: (0,) * a.ndim, pipeline_mode=pl.Buffered(1))
    lay_specs = [pl.BlockSpec((None, dil, tm // dil, ATTN_WIDTH), lambda m: (m // tps, 0, m % tps, 0))
                 for dil in DILATIONS]
    lay_shapes = [jax.ShapeDtypeStruct((bsz, dil, seq // dil, ATTN_WIDTH), BF16) for dil in DILATIONS]
    nd = len(DILATIONS)
    outs = pl.pallas_call(
        functools.partial(_inproj_kernel, tm=tm),
        grid=(T // tm,),
        in_specs=[pl.BlockSpec((tm, D_MODEL), row), const(g), const(w_bf16)],
        out_specs=[
            pl.BlockSpec((tm, S5_WIDTH), lambda m: (m % tps, m // tps)),
            pl.BlockSpec((tm, POOL_WIDTH), row),
            *(lay_specs * 3),
            pl.BlockSpec((tm, 3 * D_MODEL), row),
        ],
        out_shape=[
            jax.ShapeDtypeStruct((seq, bsz * S5_WIDTH), BF16),
            jax.ShapeDtypeStruct((T, POOL_WIDTH), F32),
            *(lay_shapes * 3),
            jax.ShapeDtypeStruct((T, 3 * D_MODEL), BF16),
        ],
        scratch_shapes=[pltpu.VMEM((ATTN_WIDTH // LANES, tm, LANES), F32)] * (nd - 1),
        compiler_params=_cparams("parallel"),
        name="inproj",
    )(h2d, g, w_bf16)
    q, k, v = (outs[2 + nd * i:2 + nd * (i + 1)] for i in range(3))
    return outs[0], outs[1], q, k, v, outs[2 + 3 * nd]


def _s5_kernel(u_ref, bu_ref, cu_ref, cb_ref, a_ref, y_ref, st_ref, tmp_ref, xe_ref, xo_ref,
               p_ref, ypar_ref, ynat_ref, *sig_refs, tt, bsz):
    d = pl.program_id(0)
    i = pl.program_id(1)
    hs = S5_UNIT_STATE
    npair = tt // 2
    prows = npair * bsz
    nu = S5_UNITS

    @pl.when(i == 0)
    def _():
        st_ref[...] = jnp.zeros_like(st_ref)

    for b in range(bsz):
        for j in range(nu):
            lanes = slice(S5_WIDTH * b + LANES * j, S5_WIDTH * b + LANES * (j + 1))
            tmp_ref[b * nu + j] = u_ref[:, lanes].astype(F32)
            xe_ref[j, pl.ds(b, npair, stride=bsz), :] = tmp_ref[b * nu + j, pl.ds(0, npair, stride=2), :]
            xo_ref[j, pl.ds(b, npair, stride=bsz), :] = tmp_ref[b * nu + j, pl.ds(1, npair, stride=2), :]

    lead = pl.multiple_of((1 - d) * bsz, bsz)
    carry_rows = pl.ds(pl.multiple_of(d * prows, bsz), bsz)
    x2 = []
    for j in range(nu):
        x2.append(jnp.concatenate([xe_ref[j], xo_ref[j]], axis=1).astype(BF16))
        sig_refs[j][pl.ds(lead, prows), :] = jnp.dot(x2[j], bu_ref[j], preferred_element_type=F32)
        sig_refs[j][carry_rows, :] = st_ref[:, 2 * hs * j:2 * hs * (j + 1)]

    for j0 in range(0, nu, 2):
        units = (j0, j0 + 1)
        coef = [(jnp.broadcast_to(a_ref[0:1, hs * j:hs * (j + 1)], (bsz, hs)),
                 jnp.broadcast_to(a_ref[1:2, hs * j:hs * (j + 1)], (bsz, hs))) for j in units]
        init = tuple((st_ref[:, 2 * hs * j:2 * hs * j + hs], st_ref[:, 2 * hs * j + hs:2 * hs * (j + 1)])
                     for j in units)

        def step(s, carry, units=units, coef=coef):
            m = jnp.where(d == 0, s, npair - 1 - s)
            rows = pl.ds(pl.multiple_of(m * bsz + lead, bsz), bsz)
            out = []
            for j, (ar, ai), (sr, si) in zip(units, coef, carry):
                sig = sig_refs[j]
                nr = ar * sr - ai * si + sig[rows, :hs]
                ni = ar * si + ai * sr + sig[rows, hs:]
                sig[rows, :hs] = nr
                sig[rows, hs:] = ni
                out.append((nr, ni))
            return tuple(out)

        final = lax.fori_loop(0, npair, step, init, unroll=2)
        for j, (sr, si) in zip(units, final):
            st_ref[:, 2 * hs * j:2 * hs * j + hs] = sr
            st_ref[:, 2 * hs * j + hs:2 * hs * (j + 1)] = si

    for j in range(nu):
        p_ref[j] = jnp.dot(sig_refs[j][...].astype(BF16), cu_ref[j], preferred_element_type=F32)
        ypar_ref[0, j] = p_ref[j, pl.ds(lead, prows), :LANES]
        ypar_ref[1, j] = (p_ref[j, pl.ds(bsz - lead, prows), LANES:]
                          + jnp.dot(x2[j], cb_ref[j], preferred_element_type=F32))

    for b in range(bsz):
        for j in range(nu):
            lanes = slice(S5_WIDTH * b + LANES * j, S5_WIDTH * b + LANES * (j + 1))
            k = b * nu + j
            ynat_ref[k, pl.ds(1 - d, npair, stride=2), :] = ypar_ref[0, j, pl.ds(b, npair, stride=bsz), :]
            ynat_ref[k, pl.ds(d, npair, stride=2), :] = ypar_ref[1, j, pl.ds(b, npair, stride=bsz), :]
            y_ref[:, lanes] = ynat_ref[k].astype(BF16)


def _s5_scan(u_tm, bu, cu, cb, a2, bsz, seq, tt=128):
    assert bsz == SUBLANES and tt % 2 == 0
    n_t = seq // tt
    tile = lambda d, i: i + d * (n_t - 1 - 2 * i)
    uw = 2 * S5_UNIT_STATE
    prows = tt // 2 * bsz
    per_dir = lambda a: pl.BlockSpec((None,) + a.shape[1:], lambda d, i: (d,) + (0,) * (a.ndim - 1))
    return pl.pallas_call(
        functools.partial(_s5_kernel, tt=tt, bsz=bsz),
        grid=(2, n_t),
        in_specs=[pl.BlockSpec((tt, bsz * S5_WIDTH), lambda d, i: (tile(d, i), 0)),
                  per_dir(bu), per_dir(cu), per_dir(cb), per_dir(a2)],
        out_specs=pl.BlockSpec((None, tt, bsz * S5_WIDTH), lambda d, i: (d, tile(d, i), 0)),
        out_shape=jax.ShapeDtypeStruct((2, seq, bsz * S5_WIDTH), BF16),
        scratch_shapes=[pltpu.VMEM((bsz, S5_UNITS * uw), F32),
                        pltpu.VMEM((bsz * S5_UNITS, tt, LANES), F32),
                        pltpu.VMEM((S5_UNITS, prows, LANES), F32),
                        pltpu.VMEM((S5_UNITS, prows, LANES), F32),
                        pltpu.VMEM((S5_UNITS, prows + bsz, 2 * LANES), F32),
                        pltpu.VMEM((2, S5_UNITS, prows, LANES), F32),
                        pltpu.VMEM((bsz * S5_UNITS, tt, LANES), F32)]
        + [pltpu.VMEM((prows + bsz, uw), F32)] * S5_UNITS,
        compiler_params=_cparams("arbitrary", "arbitrary"),
        name="s5_scan",
    )(u_tm, bu, cu, cb, a2)


def _s5_params(lam_re, lam_im, log_dt, b_re, b_im, c_re, c_im):
    f = lambda t: t.astype(F32)
    lam_re, lam_im, b_re, b_im, c_re, c_im = map(f, (lam_re, lam_im, b_re, b_im, c_re, c_im))
    dt = jnp.exp(f(log_dt))[:, :, None]
    mag = jnp.exp(lam_re * dt)
    a_re = mag * jnp.cos(lam_im * dt)
    a_im = mag * jnp.sin(lam_im * dt)
    den = lam_re * lam_re + lam_im * lam_im
    f_re = ((a_re - 1.0) * lam_re + a_im * lam_im) / den
    f_im = (a_im * lam_re - (a_re - 1.0) * lam_im) / den
    bb_re = f_re[..., None] * b_re - f_im[..., None] * b_im
    bb_im = f_re[..., None] * b_im + f_im[..., None] * b_re
    ug, nu = S5_UNIT_GROUPS, S5_UNITS
    eye = jnp.eye(ug, dtype=F32)

    def pack_b(t):
        t = t.reshape(2, nu, ug, S5_STATE, S5_GROUP_SIZE)
        t = jnp.einsum('dugnp,gh->dugphn', t, eye)
        return t.reshape(2, nu, LANES, S5_UNIT_STATE)

    def pack_c(t):
        t = t.reshape(2, nu, ug, S5_GROUP_SIZE, S5_STATE)
        t = jnp.einsum('dugpn,gh->duhngp', t, eye)
        return t.reshape(2, nu, S5_UNIT_STATE, LANES)

    def pack_pp(t):
        t = t.reshape(2, nu, ug, S5_GROUP_SIZE, S5_GROUP_SIZE)
        t = jnp.einsum('dugpq,gh->dugqhp', t, eye)
        return t.reshape(2, nu, LANES, LANES)

    ab_re = a_re[..., None] * bb_re - a_im[..., None] * bb_im
    ab_im = a_re[..., None] * bb_im + a_im[..., None] * bb_re
    b_plain = jnp.concatenate([pack_b(bb_re), pack_b(bb_im)], axis=-1)
    b_first = jnp.concatenate([pack_b(ab_re), pack_b(ab_im)], axis=-1)
    bu = jnp.stack([jnp.concatenate([b_first[0], b_plain[0]], axis=-2),
                    jnp.concatenate([b_plain[1], b_first[1]], axis=-2)]).astype(BF16)

    ca_re = c_re * a_re[:, :, None, :] - c_im * a_im[:, :, None, :]
    ca_im = c_re * a_im[:, :, None, :] + c_im * a_re[:, :, None, :]
    c_plain = jnp.concatenate([pack_c(c_re), -pack_c(c_im)], axis=-2)
    c_a = jnp.concatenate([pack_c(ca_re), -pack_c(ca_im)], axis=-2)
    cu = jnp.concatenate([c_plain, c_a], axis=-1).astype(BF16)

    cb_pq = (jnp.einsum('dgpn,dgnq->dgpq', c_re, bb_re) - jnp.einsum('dgpn,dgnq->dgpq', c_im, bb_im))
    cb_mat = pack_pp(cb_pq)
    zero = jnp.zeros_like(cb_mat[0])
    cb = jnp.stack([jnp.concatenate([cb_mat[0], zero], axis=-2),
                    jnp.concatenate([zero, cb_mat[1]], axis=-2)]).astype(BF16)

    a2_re = a_re * a_re - a_im * a_im
    a2_im = 2.0 * a_re * a_im
    a2 = jnp.stack([a2_re.reshape(2, -1), a2_im.reshape(2, -1)], axis=1)
    return bu, cu, cb, a2


POOL_PAD = 16


def _pool_kernel(u_ref, w_ref, s_ref, o_ref, xp_ref, *, seq, rows):
    pad = POOL_PAD
    zeros = jnp.zeros((pad, POOL_WIDTH), F32)
    xp_ref[0:pad, :] = zeros
    xp_ref[pad + seq:2 * pad + seq, :] = zeros
    xp_ref[pad:pad + seq, :] = u_ref[...]
    n = rows + 2 * pad

    def chunk(c, carry):
        r0 = pl.multiple_of(c * rows, rows)
        t = r0 + lax.broadcasted_iota(jnp.int32, (rows, 1), 0)
        for g, win in enumerate(POOL_WINDOWS):
            lanes = slice(POOL_GROUP_SIZE * g, POOL_GROUP_SIZE * (g + 1))
            x = xp_ref[pl.ds(r0, n), lanes]
            s = x
            k = 1
            while k < win:
                s = s + pltpu.roll(s, k, axis=0)
                k *= 2
            left = win // 2
            right = win - 1 - left
            if right:
                s = pltpu.roll(s, n - right, axis=0)
            hi = jnp.minimum(t + right + 1, seq)
            lo = jnp.maximum(t - left, 0)
            mean = s[pad:pad + rows] / (hi - lo).astype(F32)
            pooled = mean - x[pad:pad + rows]
            mixed = jnp.dot(pooled.astype(BF16), w_ref[g], preferred_element_type=F32)
            o_ref[pl.ds(r0, rows), lanes] = (mixed * s_ref[:, lanes]).astype(BF16)
        return carry

    lax.fori_loop(0, seq // rows, chunk, 0)


def _pool(u_pool, w_bf16, scale, bsz, seq, rows=256):
    return pl.pallas_call(
        functools.partial(_pool_kernel, seq=seq, rows=rows),
        grid=(bsz,),
        in_specs=[
            pl.BlockSpec((seq, POOL_WIDTH), lambda b: (b, 0)),
            pl.BlockSpec((len(POOL_WINDOWS), POOL_GROUP_SIZE, POOL_GROUP_SIZE), lambda b: (0, 0, 0)),
            pl.BlockSpec((1, POOL_WIDTH), lambda b: (0, 0)),
        ],
        out_specs=pl.BlockSpec((seq, POOL_WIDTH), lambda b: (b, 0)),
        out_shape=jax.ShapeDtypeStruct((bsz * seq, POOL_WIDTH), BF16),
        scratch_shapes=[pltpu.VMEM((seq + 2 * POOL_PAD, POOL_WIDTH), F32)],
        compiler_params=_cparams("parallel"),
        name="pool",
    )(u_pool, w_bf16, scale)


def _stat_lane(head):
    return head + HEAD_DIM * (1 - head % 2)


STAT_MAX_SHIFT = 8


def _attn_kernel(q_ref, k_ref, v_ref, bias_ref, o_ref, stat_ref, *scr, dil, m_rows, pairs, unroll):
    hc = pl.program_id(1)
    nqb = m_rows // QBLK
    stat = stat_ref if dil == 1 else scr[1]
    stat[...] = jnp.ones_like(stat)

    low = lax.broadcasted_iota(jnp.int32, (QBLK, LANES), 1) < HEAD_DIM

    def qblock(it, carry):
        r = it // nqb
        m0 = pl.multiple_of((it % nqb) * QBLK, QBLK)
        ks = pl.multiple_of(jnp.clip(m0 - QBLK // 2, 0, m_rows - KWIN), QBLK // 2)
        var = (m0 - ks) // (QBLK // 2)
        qrows = pl.ds(m0, QBLK)
        krows = pl.ds(ks, KWIN)
        srows = pl.ds(pl.multiple_of(r * m_rows + m0, QBLK), QBLK)
        for j in range(pairs):
            lanes = slice(LANES * j, LANES * (j + 1))
            qp = q_ref[r, qrows, lanes]
            zero = jnp.zeros_like(qp)
            q2 = jnp.concatenate([jnp.where(low, qp, zero), jnp.where(low, zero, qp)], axis=0)
            s2 = lax.dot_general(q2, k_ref[r, krows, lanes], (((1,), (1,)), ((), ())),
                                 preferred_element_type=F32)
            vp = v_ref[r, krows, lanes]
            halves = []
            for e in range(2):
                s = s2[QBLK * e:QBLK * (e + 1)] + bias_ref[var, 2 * (hc * pairs + j) + e]
                mx = jnp.max(s, axis=-1, keepdims=True)
                p = jnp.exp2(s - mx)
                halves.append(jnp.dot(p.astype(BF16), vp, preferred_element_type=F32))
                sl = _stat_lane(2 * j + e)
                stat[srows, sl:sl + 1] = jnp.sum(p, axis=-1, keepdims=True)
                stat[srows, sl + STAT_MAX_SHIFT:sl + STAT_MAX_SHIFT + 1] = mx
            o_pair = jnp.where(low, halves[0], halves[1])
            if dil == 1:
                o_ref[qrows, lanes] = o_pair.astype(BF16)
            else:
                scr[0][j, pl.ds(r + dil * m0, QBLK, stride=dil), :] = o_pair
        return carry

    lax.fori_loop(0, dil * nqb, qblock, 0, unroll=unroll)
    if dil > 1:
        for j in range(pairs):
            o_ref[:, LANES * j:LANES * (j + 1)] = scr[0][j].astype(BF16)
        for r in range(dil):
            rows = slice(r * m_rows, (r + 1) * m_rows)
            stat_ref[pl.ds(r, m_rows, stride=dil), :] = stat[rows, :]


def _attn_bias(dil):
    slopes = jnp.exp2(-8.0 * jnp.arange(1, ATTN_HEADS + 1, dtype=F32) / ATTN_HEADS)
    a = jnp.arange(QBLK)[:, None]
    c = jnp.arange(KWIN)[None, :]
    off = (jnp.arange(3) * (QBLK // 2))[:, None, None]
    rel = c[None] - off - a[None]
    dist = (dil * jnp.abs(rel)).astype(F32)
    bias = -LOG2E * slopes[None, :, None, None] * dist[:, None]
    return jnp.where((jnp.abs(rel) <= QBLK // 2)[:, None], bias, NEG_INF)


def _attn_pattern(q, k, v, bias, dil, bsz, seq, hw=ATTN_CHUNK, unroll=8):
    m_rows = seq // dil
    nhc = ATTN_WIDTH // hw
    pairs = hw // LANES
    blk = pl.BlockSpec((None, dil, m_rows, hw), lambda b, h: (b, 0, 0, h))
    stat = pl.BlockSpec((seq, LANES), lambda b, h: (b, h))
    regroup = [pltpu.VMEM((pairs, seq, LANES), F32), pltpu.VMEM((seq, LANES), F32)]
    return pl.pallas_call(
        functools.partial(_attn_kernel, dil=dil, m_rows=m_rows, pairs=pairs, unroll=unroll),
        grid=(bsz, nhc),
        in_specs=[blk, blk, blk,
                  pl.BlockSpec(bias.shape, lambda b, h: (0, 0, 0, 0), pipeline_mode=pl.Buffered(1))],
        out_specs=[pl.BlockSpec((seq, hw), lambda b, h: (b, h)), stat],
        out_shape=[jax.ShapeDtypeStruct((bsz * seq, ATTN_WIDTH), BF16),
                   jax.ShapeDtypeStruct((bsz * seq, nhc * LANES), F32)],
        scratch_shapes=regroup if dil > 1 else [],
        compiler_params=_cparams("parallel", "parallel"),
        name=f"attn_d{dil}",
    )(q, k, v, bias)


def _merge_kernel(yf_ref, yb_ref, us5_ref, dsk_ref, wglu_ref, ypool_ref,
                  o1_ref, o2_ref, o3_ref, s1_ref, s2_ref, s3_ref, exp_ref,
                  gate_ref, h_ref, wbs_ref, wbp_ref, wba_ref, wout_ref, out_ref):
    dot = functools.partial(jnp.dot, preferred_element_type=F32)
    y = (yf_ref[...].astype(F32) + yb_ref[...].astype(F32)
         + dsk_ref[...] * us5_ref[...].astype(F32))
    y = jax.nn.gelu(y)
    y = y * _sigmoid(dot(y.astype(BF16), wglu_ref[...]))
    br_s5 = dot(y.astype(BF16), wbs_ref[...])
    br_pool = dot(ypool_ref[...], wbp_ref[...])

    lane = lax.broadcasted_iota(jnp.int32, (h_ref.shape[0], LANES), 1)
    is_stat = functools.reduce(jnp.logical_or,
                               [lane == _stat_lane(hd) for hd in range(ATTN_CHUNK // HEAD_DIM)])
    att = []
    for c in range(ATTN_NCHUNK):
        st, ch = slice(LANES * c, LANES * (c + 1)), slice(ATTN_CHUNK * c, ATTN_CHUNK * (c + 1))
        d1, d2, d3 = s1_ref[:, st], s2_ref[:, st], s3_ref[:, st]
        m1, m2, m3 = (pltpu.roll(s, LANES - STAT_MAX_SHIFT, axis=1) for s in (d1, d2, d3))
        mx = jnp.maximum(jnp.maximum(m1, m2), m3)
        w1, w2, w3 = jnp.exp2(m1 - mx), jnp.exp2(m2 - mx), jnp.exp2(m3 - mx)
        inv = 1.0 / (w1 * d1 + w2 * d2 + w3 * d3)
        expand = lambda w: dot(jnp.where(is_stat, w * inv, 0.0).astype(BF16), exp_ref[...])
        att.append(expand(w1) * o1_ref[:, ch].astype(F32) + expand(w2) * o2_ref[:, ch].astype(F32)
                   + expand(w3) * o3_ref[:, ch].astype(F32))
    br_attn = dot(jnp.concatenate(att, axis=1).astype(BF16), wba_ref[...])

    merged = (gate_ref[:, 0:D_MODEL].astype(F32) * br_s5
              + gate_ref[:, D_MODEL:2 * D_MODEL].astype(F32) * br_pool
              + gate_ref[:, 2 * D_MODEL:3 * D_MODEL].astype(F32) * br_attn)
    out_ref[...] = h_ref[...] + dot(merged.astype(BF16), wout_ref[...])


def _merge(y_dirs, us5_tm, dsk, wglu, ypool, outs, stats, gates, h2d, wbs, wbp, wba, wout,
           bsz, seq, tm=512):
    T = bsz * seq
    tps = seq // tm
    row = lambda w: pl.BlockSpec((tm, w), lambda b, i: (b * tps + i, 0))
    full = lambda a: pl.BlockSpec(a.shape, lambda b, i: (0,) * a.ndim)
    ydir = lambda d: pl.BlockSpec((None, tm, S5_WIDTH), lambda b, i: (d, i, b))
    local_head = jnp.arange(ATTN_CHUNK)[None, :] // HEAD_DIM
    expand = (jnp.arange(LANES)[:, None] == _stat_lane(local_head)).astype(BF16)
    return pl.pallas_call(
        _merge_kernel,
        grid=(bsz, tps),
        in_specs=[ydir(0), ydir(1),
                  pl.BlockSpec((tm, S5_WIDTH), lambda b, i: (i, b)),
                  full(dsk), full(wglu), row(POOL_WIDTH),
                  row(ATTN_WIDTH), row(ATTN_WIDTH), row(ATTN_WIDTH),
                  *([row(ATTN_NCHUNK * LANES)] * 3), full(expand),
                  row(3 * D_MODEL), row(D_MODEL),
                  full(wbs), full(wbp), full(wba), full(wout)],
        out_specs=row(D_MODEL),
        out_shape=jax.ShapeDtypeStruct((T, D_MODEL), F32),
        compiler_params=_cparams("parallel", "parallel"),
        name="merge",
    )(y_dirs, y_dirs, us5_tm, dsk, wglu, ypool, *outs, *stats, expand, gates, h2d,
      wbs, wbp, wba, wout)


def _mlp_kernel(h_ref, g_ref, wup_ref, wdn_ref, gf_ref, o_ref, *, final, chunk):
    h = h_ref[...]
    hn = _rms(h, g_ref[...]).astype(BF16)
    acc = h
    for c in range(D_FF // chunk):
        cols = slice(chunk * c, chunk * (c + 1))
        up = jnp.dot(hn, wup_ref[:, cols], preferred_element_type=F32)
        act = jnp.square(jnp.maximum(up, 0.0)).astype(BF16)
        acc = acc + jnp.dot(act, wdn_ref[cols, :], preferred_element_type=F32)
    if final:
        acc = _rms(acc, gf_ref[...])
    o_ref[...] = acc


def _mlp(h2d, g, wup, wdn, gf, final, tm=1024, chunk=1024):
    T = h2d.shape[0]
    const = lambda a: pl.BlockSpec(a.shape, lambda m: (0,) * a.ndim, pipeline_mode=pl.Buffered(1))
    return pl.pallas_call(
        functools.partial(_mlp_kernel, final=final, chunk=chunk),
        grid=(T // tm,),
        in_specs=[pl.BlockSpec((tm, D_MODEL), lambda m: (m, 0)),
                  const(g), const(wup), const(wdn), const(gf)],
        out_specs=pl.BlockSpec((tm, D_MODEL), lambda m: (m, 0)),
        out_shape=jax.ShapeDtypeStruct((T, D_MODEL), F32),
        compiler_params=_cparams("parallel"),
        name="mlp",
    )(h2d, g, wup, wdn, gf)


def kernel(x, norm_mix, w_in, s5_lam_re, s5_lam_im, s5_log_dt, s5_b_re, s5_b_im, s5_c_re, s5_c_im, s5_d, s5_w_glu, pool_w, pool_scale, w_branch_s5, w_branch_pool, w_branch_attn, w_out, norm_mlp, w_up, w_down, norm_final):
    bsz, seq, _ = x.shape
    depth = w_in.shape[0]
    bf = lambda t: t.astype(BF16)
    row = lambda t: t.astype(F32).reshape(1, -1)
    h = x.astype(F32).reshape(bsz * seq, D_MODEL)
    s5_w = jax.vmap(_s5_params)(s5_lam_re, s5_lam_im, s5_log_dt, s5_b_re, s5_b_im, s5_c_re, s5_c_im)
    biases = [_attn_bias(dil) for dil in DILATIONS]
    for l in range(depth):
        us5_tm, u_pool, q, k, v, gates = _inproj(h, row(norm_mix[l]), bf(w_in[l]), bsz, seq)
        y_dirs = _s5_scan(us5_tm, *(w[l] for w in s5_w), bsz, seq)
        y_pool = _pool(u_pool, bf(pool_w[l]), row(pool_scale[l]), bsz, seq)
        outs, stats = zip(*[_attn_pattern(q[i], k[i], v[i], biases[i], dil, bsz, seq)
                            for i, dil in enumerate(DILATIONS)])
        h = _merge(y_dirs, us5_tm, row(s5_d[l]), bf(s5_w_glu[l]), y_pool, outs, stats, gates, h,
                   bf(w_branch_s5[l]), bf(w_branch_pool[l]), bf(w_branch_attn[l]), bf(w_out[l]),
                   bsz, seq)
        h = _mlp(h, row(norm_mlp[l]), bf(w_up[l]), bf(w_down[l]), row(norm_final), l == depth - 1)
    return h.reshape(bsz, seq, D_MODEL).astype(x.dtype)
```

```python
import functools
import math

import jax
import jax.numpy as jnp
from jax import lax
from jax.experimental import pallas as pl
from jax.experimental.pallas import tpu as pltpu

F32 = jnp.float32
BF16 = jnp.bfloat16

D_MODEL = 1024
S5_WIDTH = 512
S5_GROUP_SIZE = 16
S5_STATE = 64
POOL_WIDTH = 512
POOL_WINDOWS = (2, 4, 8, 16)
POOL_GROUP_SIZE = 128
ATTN_HEADS = 16
HEAD_DIM = 64
ATTN_WIDTH = 1024
ATTN_PATTERNS = ((128, 1), (512, 4), (2048, 16))
D_FF = 4096
NORM_EPS = 1e-6
NEG_INF = -1e30

LANES = 128
SUBLANES = 8
S5_UNIT_GROUPS = LANES // S5_GROUP_SIZE
S5_UNITS = S5_WIDTH // LANES
S5_UNIT_STATE = S5_UNIT_GROUPS * S5_STATE
QBLK = 128
KWIN = 2 * QBLK
ATTN_CHUNK = 256
ATTN_NCHUNK = ATTN_WIDTH // ATTN_CHUNK
VMEM_LIMIT = 56 * 1024 * 1024


def _cparams(*sem):
    return pltpu.CompilerParams(dimension_semantics=sem, vmem_limit_bytes=VMEM_LIMIT)


def _rms(x, g):
    ms = jnp.mean(x * x, axis=-1, keepdims=True)
    return x * lax.rsqrt(ms + NORM_EPS) * g


def _sigmoid(x):
    return 0.5 + 0.5 * jnp.tanh(0.5 * x)


DILATIONS = tuple(dil for _, dil in ATTN_PATTERNS)
LOG2E = math.log2(math.e)
Q_SCALE = HEAD_DIM ** -0.5 * LOG2E


def _inproj_kernel(x_ref, g_ref, w_ref, us5_ref, upool_ref, *rest, tm):
    nd = len(DILATIONS)
    qkv_refs = [rest[nd * i:nd * (i + 1)] for i in range(3)]
    gate_ref = rest[3 * nd]
    lay_refs = dict(zip(DILATIONS[:-1], rest[3 * nd + 1:]))
    xn = _rms(x_ref[...], g_ref[...]).astype(BF16)
    proj = lambda c: jnp.dot(xn, w_ref[:, D_MODEL * c:D_MODEL * (c + 1)], preferred_element_type=F32)

    acc = proj(0)
    us5_ref[...] = acc[:, :S5_WIDTH].astype(BF16)
    upool_ref[...] = acc[:, S5_WIDTH:]

    for i, refs in enumerate(qkv_refs):
        acc = proj(1 + i)
        if i == 0:
            acc = acc * Q_SCALE
        refs[0][0] = acc.astype(BF16)
        for j in range(ATTN_WIDTH // LANES):
            lay_refs[1][j] = acc[:, LANES * j:LANES * (j + 1)]
        for base, dil, ref in zip(DILATIONS[:-1], DILATIONS[1:], refs[1:]):
            ratio, rows = dil // base, tm // dil
            for c in range(base):
                for a in range(ratio):
                    r = c + base * a
                    for j in range(ATTN_WIDTH // LANES):
                        piece = lay_refs[base][j, pl.ds(c * (tm // base) + a, rows, stride=ratio), :]
                        ref[r, :, LANES * j:LANES * (j + 1)] = piece.astype(BF16)
                        if dil in lay_refs:
                            lay_refs[dil][j, r * rows:(r + 1) * rows, :] = piece

    for c in range(3):
        gate_ref[:, D_MODEL * c:D_MODEL * (c + 1)] = _sigmoid(proj(4 + c)).astype(BF16)


def _inproj(h2d, g, w_bf16, bsz, seq, tm=512):
    T = bsz * seq
    tps = seq // tm
    row = lambda m: (m, 0)
    const = lambda a: pl.BlockSpec(a.shape, lambda m: (0,) * a.ndim, pipeline_mode=pl.Buffered(1))
    lay_specs = [pl.BlockSpec((None, dil, tm // dil, ATTN_WIDTH), lambda m: (m // tps, 0, m % tps, 0))
                 for dil in DILATIONS]
    lay_shapes = [jax.ShapeDtypeStruct((bsz, dil, seq // dil, ATTN_WIDTH), BF16) for dil in DILATIONS]
    nd = len(DILATIONS)
    outs = pl.pallas_call(
        functools.partial(_inproj_kernel, tm=tm),
        grid=(T // tm,),
        in_specs=[pl.BlockSpec((tm, D_MODEL), row), const(g), const(w_bf16)],
        out_specs=[
            pl.BlockSpec((tm, S5_WIDTH), lambda m: (m % tps, m // tps)),
            pl.BlockSpec((tm, POOL_WIDTH), row),
            *(lay_specs * 3),
            pl.BlockSpec((tm, 3 * D_MODEL), row),
        ],
        out_shape=[
            jax.ShapeDtypeStruct((seq, bsz * S5_WIDTH), BF16),
            jax.ShapeDtypeStruct((T, POOL_WIDTH), F32),
            *(lay_shapes * 3),
            jax.ShapeDtypeStruct((T, 3 * D_MODEL), BF16),
        ],
        scratch_shapes=[pltpu.VMEM((ATTN_WIDTH // LANES, tm, LANES), F32)] * (nd - 1),
        compiler_params=_cparams("parallel"),
        name="inproj",
    )(h2d, g, w_bf16)
    q, k, v = (outs[2 + nd * i:2 + nd * (i + 1)] for i in range(3))
    return outs[0], outs[1], q, k, v, outs[2 + 3 * nd]


def _s5_kernel(u_ref, bu_ref, cu_ref, cb_ref, a_ref, y_ref, st_ref, tmp_ref, xe_ref, xo_ref,
               p_ref, ypar_ref, ynat_ref, *sig_refs, tt, bsz):
    d = pl.program_id(0)
    i = pl.program_id(1)
    hs = S5_UNIT_STATE
    npair = tt // 2
    prows = npair * bsz
    nu = S5_UNITS

    @pl.when(i == 0)
    def _():
        st_ref[...] = jnp.zeros_like(st_ref)

    for b in range(bsz):
        for j in range(nu):
            lanes = slice(S5_WIDTH * b + LANES * j, S5_WIDTH * b + LANES * (j + 1))
            tmp_ref[b * nu + j] = u_ref[:, lanes].astype(F32)
            xe_ref[j, pl.ds(b, npair, stride=bsz), :] = tmp_ref[b * nu + j, pl.ds(0, npair, stride=2), :]
            xo_ref[j, pl.ds(b, npair, stride=bsz), :] = tmp_ref[b * nu + j, pl.ds(1, npair, stride=2), :]

    lead = pl.multiple_of((1 - d) * bsz, bsz)
    carry_rows = pl.ds(pl.multiple_of(d * prows, bsz), bsz)
    x2 = []
    for j in range(nu):
        x2.append(jnp.concatenate([xe_ref[j], xo_ref[j]], axis=1).astype(BF16))
        sig_refs[j][pl.ds(lead, prows), :] = jnp.dot(x2[j], bu_ref[j], preferred_element_type=F32)
        sig_refs[j][carry_rows, :] = st_ref[:, 2 * hs * j:2 * hs * (j + 1)]

    for j0 in range(0, nu, 2):
        units = (j0, j0 + 1)
        coef = [(jnp.broadcast_to(a_ref[0:1, hs * j:hs * (j + 1)], (bsz, hs)),
                 jnp.broadcast_to(a_ref[1:2, hs * j:hs * (j + 1)], (bsz, hs))) for j in units]
        init = tuple((st_ref[:, 2 * hs * j:2 * hs * j + hs], st_ref[:, 2 * hs * j + hs:2 * hs * (j + 1)])
                     for j in units)

        def step(s, carry, units=units, coef=coef):
            m = jnp.where(d == 0, s, npair - 1 - s)
            rows = pl.ds(pl.multiple_of(m * bsz + lead, bsz), bsz)
            out = []
            for j, (ar, ai), (sr, si) in zip(units, coef, carry):
                sig = sig_refs[j]
                nr = ar * sr - ai * si + sig[rows, :hs]
                ni = ar * si + ai * sr + sig[rows, hs:]
                sig[rows, :hs] = nr
                sig[rows, hs:] = ni
                out.append((nr, ni))
            return tuple(out)

        final = lax.fori_loop(0, npair, step, init, unroll=4)
        for j, (sr, si) in zip(units, final):
            st_ref[:, 2 * hs * j:2 * hs * j + hs] = sr
            st_ref[:, 2 * hs * j + hs:2 * hs * (j + 1)] = si

    for j in range(nu):
        p_ref[j] = jnp.dot(sig_refs[j][...].astype(BF16), cu_ref[j], preferred_element_type=F32)
        ypar_ref[0, j] = p_ref[j, pl.ds(lead, prows), :LANES]
        ypar_ref[1, j] = (p_ref[j, pl.ds(bsz - lead, prows), LANES:]
                          + jnp.dot(x2[j], cb_ref[j], preferred_element_type=F32))

    for b in range(bsz):
        for j in range(nu):
            lanes = slice(S5_WIDTH * b + LANES * j, S5_WIDTH * b + LANES * (j + 1))
            k = b * nu + j
            ynat_ref[k, pl.ds(1 - d, npair, stride=2), :] = ypar_ref[0, j, pl.ds(b, npair, stride=bsz), :]
            ynat_ref[k, pl.ds(d, npair, stride=2), :] = ypar_ref[1, j, pl.ds(b, npair, stride=bsz), :]
            y_ref[:, lanes] = ynat_ref[k].astype(BF16)


def _s5_scan(u_tm, bu, cu, cb, a2, bsz, seq, tt=128):
    assert bsz == SUBLANES and tt % 2 == 0
    n_t = seq // tt
    tile = lambda d, i: i + d * (n_t - 1 - 2 * i)
    uw = 2 * S5_UNIT_STATE
    prows = tt // 2 * bsz
    per_dir = lambda a: pl.BlockSpec((None,) + a.shape[1:], lambda d, i: (d,) + (0,) * (a.ndim - 1))
    return pl.pallas_call(
        functools.partial(_s5_kernel, tt=tt, bsz=bsz),
        grid=(2, n_t),
        in_specs=[pl.BlockSpec((tt, bsz * S5_WIDTH), lambda d, i: (tile(d, i), 0)),
                  per_dir(bu), per_dir(cu), per_dir(cb), per_dir(a2)],
        out_specs=pl.BlockSpec((None, tt, bsz * S5_WIDTH), lambda d, i: (d, tile(d, i), 0)),
        out_shape=jax.ShapeDtypeStruct((2, seq, bsz * S5_WIDTH), BF16),
        scratch_shapes=[pltpu.VMEM((bsz, S5_UNITS * uw), F32),
                        pltpu.VMEM((bsz * S5_UNITS, tt, LANES), F32),
                        pltpu.VMEM((S5_UNITS, prows, LANES), F32),
                        pltpu.VMEM((S5_UNITS, prows, LANES), F32),
                        pltpu.VMEM((S5_UNITS, prows + bsz, 2 * LANES), F32),
                        pltpu.VMEM((2, S5_UNITS, prows, LANES), F32),
                        pltpu.VMEM((bsz * S5_UNITS, tt, LANES), F32)]
        + [pltpu.VMEM((prows + bsz, uw), F32)] * S5_UNITS,
        compiler_params=_cparams("arbitrary", "arbitrary"),
        name="s5_scan",
    )(u_tm, bu, cu, cb, a2)


def _s5_params(lam_re, lam_im, log_dt, b_re, b_im, c_re, c_im):
    f = lambda t: t.astype(F32)
    lam_re, lam_im, b_re, b_im, c_re, c_im = map(f, (lam_re, lam_im, b_re, b_im, c_re, c_im))
    dt = jnp.exp(f(log_dt))[:, :, None]
    mag = jnp.exp(lam_re * dt)
    a_re = mag * jnp.cos(lam_im * dt)
    a_im = mag * jnp.sin(lam_im * dt)
    den = lam_re * lam_re + lam_im * lam_im
    f_re = ((a_re - 1.0) * lam_re + a_im * lam_im) / den
    f_im = (a_im * lam_re - (a_re - 1.0) * lam_im) / den
    bb_re = f_re[..., None] * b_re - f_im[..., None] * b_im
    bb_im = f_re[..., None] * b_im + f_im[..., None] * b_re
    ug, nu = S5_UNIT_GROUPS, S5_UNITS
    eye = jnp.eye(ug, dtype=F32)

    def pack_b(t):
        t = t.reshape(2, nu, ug, S5_STATE, S5_GROUP_SIZE)
        t = jnp.einsum('dugnp,gh->dugphn', t, eye)
        return t.reshape(2, nu, LANES, S5_UNIT_STATE)

    def pack_c(t):
        t = t.reshape(2, nu, ug, S5_GROUP_SIZE, S5_STATE)
        t = jnp.einsum('dugpn,gh->duhngp', t, eye)
        return t.reshape(2, nu, S5_UNIT_STATE, LANES)

    def pack_pp(t):
        t = t.reshape(2, nu, ug, S5_GROUP_SIZE, S5_GROUP_SIZE)
        t = jnp.einsum('dugpq,gh->dugqhp', t, eye)
        return t.reshape(2, nu, LANES, LANES)

    ab_re = a_re[..., None] * bb_re - a_im[..., None] * bb_im
    ab_im = a_re[..., None] * bb_im + a_im[..., None] * bb_re
    b_plain = jnp.concatenate([pack_b(bb_re), pack_b(bb_im)], axis=-1)
    b_first = jnp.concatenate([pack_b(ab_re), pack_b(ab_im)], axis=-1)
    bu = jnp.stack([jnp.concatenate([b_first[0], b_plain[0]], axis=-2),
                    jnp.concatenate([b_plain[1], b_first[1]], axis=-2)]).astype(BF16)

    ca_re = c_re * a_re[:, :, None, :] - c_im * a_im[:, :, None, :]
    ca_im = c_re * a_im[:, :, None, :] + c_im * a_re[:, :, None, :]
    c_plain = jnp.concatenate([pack_c(c_re), -pack_c(c_im)], axis=-2)
    c_a = jnp.concatenate([pack_c(ca_re), -pack_c(ca_im)], axis=-2)
    cu = jnp.concatenate([c_plain, c_a], axis=-1).astype(BF16)

    cb_pq = (jnp.einsum('dgpn,dgnq->dgpq', c_re, bb_re) - jnp.einsum('dgpn,dgnq->dgpq', c_im, bb_im))
    cb_mat = pack_pp(cb_pq)
    zero = jnp.zeros_like(cb_mat[0])
    cb = jnp.stack([jnp.concatenate([cb_mat[0], zero], axis=-2),
                    jnp.concatenate([zero, cb_mat[1]], axis=-2)]).astype(BF16)

    a2_re = a_re * a_re - a_im * a_im
    a2_im = 2.0 * a_re * a_im
    a2 = jnp.stack([a2_re.reshape(2, -1), a2_im.reshape(2, -1)], axis=1)
    return bu, cu, cb, a2


POOL_PAD = 16


def _pool_kernel(u_ref, w_ref, s_ref, o_ref, xp_ref, *, seq, rows):
    pad = POOL_PAD
    zeros = jnp.zeros((pad, POOL_WIDTH), F32)
    xp_ref[0:pad, :] = zeros
    xp_ref[pad + seq:2 * pad + seq, :] = zeros
    xp_ref[pad:pad + seq, :] = u_ref[...]
    n = rows + 2 * pad

    def chunk(c, carry):
        r0 = pl.multiple_of(c * rows, rows)
        t = r0 + lax.broadcasted_iota(jnp.int32, (rows, 1), 0)
        for g, win in enumerate(POOL_WINDOWS):
            lanes = slice(POOL_GROUP_SIZE * g, POOL_GROUP_SIZE * (g + 1))
            x = xp_ref[pl.ds(r0, n), lanes]
            s = x
            k = 1
            while k < win:
                s = s + pltpu.roll(s, k, axis=0)
                k *= 2
            left = win // 2
            right = win - 1 - left
            if right:
                s = pltpu.roll(s, n - right, axis=0)
            hi = jnp.minimum(t + right + 1, seq)
            lo = jnp.maximum(t - left, 0)
            mean = s[pad:pad + rows] / (hi - lo).astype(F32)
            pooled = mean - x[pad:pad + rows]
            mixed = jnp.dot(pooled.astype(BF16), w_ref[g], preferred_element_type=F32)
            o_ref[pl.ds(r0, rows), lanes] = (mixed * s_ref[:, lanes]).astype(BF16)
        return carry

    lax.fori_loop(0, seq // rows, chunk, 0, unroll=2)


def _pool(u_pool, w_bf16, scale, bsz, seq, rows=256):
    return pl.pallas_call(
        functools.partial(_pool_kernel, seq=seq, rows=rows),
        grid=(bsz,),
        in_specs=[
            pl.BlockSpec((seq, POOL_WIDTH), lambda b: (b, 0)),
            pl.BlockSpec((len(POOL_WINDOWS), POOL_GROUP_SIZE, POOL_GROUP_SIZE), lambda b: (0, 0, 0)),
            pl.BlockSpec((1, POOL_WIDTH), lambda b: (0, 0)),
        ],
        out_specs=pl.BlockSpec((seq, POOL_WIDTH), lambda b: (b, 0)),
        out_shape=jax.ShapeDtypeStruct((bsz * seq, POOL_WIDTH), BF16),
        scratch_shapes=[pltpu.VMEM((seq + 2 * POOL_PAD, POOL_WIDTH), F32)],
        compiler_params=_cparams("parallel"),
        name="pool",
    )(u_pool, w_bf16, scale)


def _stat_lane(head):
    return head + HEAD_DIM * (1 - head % 2)


STAT_MAX_SHIFT = 8


def _attn_kernel(q_ref, k_ref, v_ref, bias_ref, o_ref, stat_ref, *scr, dil, m_rows, pairs, unroll):
    hc = pl.program_id(1)
    nqb = m_rows // QBLK
    stat = stat_ref if dil == 1 else scr[1]
    stat[...] = jnp.ones_like(stat)

    low = lax.broadcasted_iota(jnp.int32, (QBLK, LANES), 1) < HEAD_DIM

    def qblock(it, carry):
        r = it // nqb
        m0 = pl.multiple_of((it % nqb) * QBLK, QBLK)
        ks = pl.multiple_of(jnp.clip(m0 - QBLK // 2, 0, m_rows - KWIN), QBLK // 2)
        var = (m0 - ks) // (QBLK // 2)
        qrows = pl.ds(m0, QBLK)
        krows = pl.ds(ks, KWIN)
        srows = pl.ds(pl.multiple_of(r * m_rows + m0, QBLK), QBLK)
        for j in range(pairs):
            lanes = slice(LANES * j, LANES * (j + 1))
            qp = q_ref[r, qrows, lanes]
            zero = jnp.zeros_like(qp)
            q2 = jnp.concatenate([jnp.where(low, qp, zero), jnp.where(low, zero, qp)], axis=0)
            s2 = lax.dot_general(q2, k_ref[r, krows, lanes], (((1,), (1,)), ((), ())),
                                 preferred_element_type=F32)
            vp = v_ref[r, krows, lanes]
            halves = []
            for e in range(2):
                s = s2[QBLK * e:QBLK * (e + 1)] + bias_ref[var, 2 * (hc * pairs + j) + e]
                mx = jnp.max(s, axis=-1, keepdims=True)
                p = jnp.exp2(s - mx)
                halves.append(jnp.dot(p.astype(BF16), vp, preferred_element_type=F32))
                sl = _stat_lane(2 * j + e)
                stat[srows, sl:sl + 1] = jnp.sum(p, axis=-1, keepdims=True)
                stat[srows, sl + STAT_MAX_SHIFT:sl + STAT_MAX_SHIFT + 1] = mx
            o_pair = jnp.where(low, halves[0], halves[1])
            if dil == 1:
                o_ref[qrows, lanes] = o_pair.astype(BF16)
            else:
                scr[0][j, pl.ds(r + dil * m0, QBLK, stride=dil), :] = o_pair
        return carry

    lax.fori_loop(0, dil * nqb, qblock, 0, unroll=unroll)
    if dil > 1:
        for j in range(pairs):
            o_ref[:, LANES * j:LANES * (j + 1)] = scr[0][j].astype(BF16)
        for r in range(dil):
            rows = slice(r * m_rows, (r + 1) * m_rows)
            stat_ref[pl.ds(r, m_rows, stride=dil), :] = stat[rows, :]


def _attn_bias(dil):
    slopes = jnp.exp2(-8.0 * jnp.arange(1, ATTN_HEADS + 1, dtype=F32) / ATTN_HEADS)
    a = jnp.arange(QBLK)[:, None]
    c = jnp.arange(KWIN)[None, :]
    off = (jnp.arange(3) * (QBLK // 2))[:, None, None]
    rel = c[None] - off - a[None]
    dist = (dil * jnp.abs(rel)).astype(F32)
    bias = -LOG2E * slopes[None, :, None, None] * dist[:, None]
    return jnp.where((jnp.abs(rel) <= QBLK // 2)[:, None], bias, NEG_INF)


def _attn_pattern(q, k, v, bias, dil, bsz, seq, hw=ATTN_CHUNK, unroll=32):
    m_rows = seq // dil
    nhc = ATTN_WIDTH // hw
    pairs = hw // LANES
    blk = pl.BlockSpec((None, dil, m_rows, hw), lambda b, h: (b, 0, 0, h))
    stat = pl.BlockSpec((seq, LANES), lambda b, h: (b, h))
    regroup = [pltpu.VMEM((pairs, seq, LANES), F32), pltpu.VMEM((seq, LANES), F32)]
    return pl.pallas_call(
        functools.partial(_attn_kernel, dil=dil, m_rows=m_rows, pairs=pairs, unroll=unroll),
        grid=(bsz, nhc),
        in_specs=[blk, blk, blk,
                  pl.BlockSpec(bias.shape, lambda b, h: (0, 0, 0, 0), pipeline_mode=pl.Buffered(1))],
        out_specs=[pl.BlockSpec((seq, hw), lambda b, h: (b, h)), stat],
        out_shape=[jax.ShapeDtypeStruct((bsz * seq, ATTN_WIDTH), BF16),
                   jax.ShapeDtypeStruct((bsz * seq, nhc * LANES), F32)],
        scratch_shapes=regroup if dil > 1 else [],
        compiler_params=_cparams("parallel", "parallel"),
        name=f"attn_d{dil}",
    )(q, k, v, bias)


def _merge_kernel(yf_ref, yb_ref, us5_ref, dsk_ref, wglu_ref, ypool_ref,
                  o1_ref, o2_ref, o3_ref, s1_ref, s2_ref, s3_ref, exp_ref,
                  gate_ref, h_ref, wbs_ref, wbp_ref, wba_ref, wout_ref, out_ref):
    dot = functools.partial(jnp.dot, preferred_element_type=F32)
    y = (yf_ref[...].astype(F32) + yb_ref[...].astype(F32)
         + dsk_ref[...] * us5_ref[...].astype(F32))
    y = jax.nn.gelu(y)
    y = y * _sigmoid(dot(y.astype(BF16), wglu_ref[...]))
    br_s5 = dot(y.astype(BF16), wbs_ref[...])
    br_pool = dot(ypool_ref[...], wbp_ref[...])

    lane = lax.broadcasted_iota(jnp.int32, (h_ref.shape[0], LANES), 1)
    is_stat = functools.reduce(jnp.logical_or,
                               [lane == _stat_lane(hd) for hd in range(ATTN_CHUNK // HEAD_DIM)])
    att = []
    for c in range(ATTN_NCHUNK):
        st, ch = slice(LANES * c, LANES * (c + 1)), slice(ATTN_CHUNK * c, ATTN_CHUNK * (c + 1))
        d1, d2, d3 = s1_ref[:, st], s2_ref[:, st], s3_ref[:, st]
        m1, m2, m3 = (pltpu.roll(s, LANES - STAT_MAX_SHIFT, axis=1) for s in (d1, d2, d3))
        mx = jnp.maximum(jnp.maximum(m1, m2), m3)
        w1, w2, w3 = jnp.exp2(m1 - mx), jnp.exp2(m2 - mx), jnp.exp2(m3 - mx)
        inv = 1.0 / (w1 * d1 + w2 * d2 + w3 * d3)
        expand = lambda w: dot(jnp.where(is_stat, w * inv, 0.0).astype(BF16), exp_ref[...])
        att.append(expand(w1) * o1_ref[:, ch].astype(F32) + expand(w2) * o2_ref[:, ch].astype(F32)
                   + expand(w3) * o3_ref[:, ch].astype(F32))
    br_attn = dot(jnp.concatenate(att, axis=1).astype(BF16), wba_ref[...])

    merged = (gate_ref[:, 0:D_MODEL].astype(F32) * br_s5
              + gate_ref[:, D_MODEL:2 * D_MODEL].astype(F32) * br_pool
              + gate_ref[:, 2 * D_MODEL:3 * D_MODEL].astype(F32) * br_attn)
    out_ref[...] = h_ref[...] + dot(merged.astype(BF16), wout_ref[...])


def _merge(y_dirs, us5_tm, dsk, wglu, ypool, outs, stats, gates, h2d, wbs, wbp, wba, wout,
           bsz, seq, tm=512):
    T = bsz * seq
    tps = seq // tm
    row = lambda w: pl.BlockSpec((tm, w), lambda b, i: (b * tps + i, 0))
    full = lambda a: pl.BlockSpec(a.shape, lambda b, i: (0,) * a.ndim)
    ydir = lambda d: pl.BlockSpec((None, tm, S5_WIDTH), lambda b, i: (d, i, b))
    local_head = jnp.arange(ATTN_CHUNK)[None, :] // HEAD_DIM
    expand = (jnp.arange(LANES)[:, None] == _stat_lane(local_head)).astype(BF16)
    return pl.pallas_call(
        _merge_kernel,
        grid=(bsz, tps),
        in_specs=[ydir(0), ydir(1),
                  pl.BlockSpec((tm, S5_WIDTH), lambda b, i: (i, b)),
                  full(dsk), full(wglu), row(POOL_WIDTH),
                  row(ATTN_WIDTH), row(ATTN_WIDTH), row(ATTN_WIDTH),
                  *([row(ATTN_NCHUNK * LANES)] * 3), full(expand),
                  row(3 * D_MODEL), row(D_MODEL),
                  full(wbs), full(wbp), full(wba), full(wout)],
        out_specs=row(D_MODEL),
        out_shape=jax.ShapeDtypeStruct((T, D_MODEL), F32),
        compiler_params=_cparams("parallel", "parallel"),
        name="merge",
    )(y_dirs, y_dirs, us5_tm, dsk, wglu, ypool, *outs, *stats, expand, gates, h2d,
      wbs, wbp, wba, wout)


def _mlp_kernel(h_ref, g_ref, wup_ref, wdn_ref, gf_ref, o_ref, *, final, chunk):
    h = h_ref[...]
    hn = _rms(h, g_ref[...]).astype(BF16)
    acc = h
    for c in range(D_FF // chunk):
        cols = slice(chunk * c, chunk * (c + 1))
        up = jnp.dot(hn, wup_ref[:, cols], preferred_element_type=F32)
        act = jnp.square(jnp.maximum(up, 0.0)).astype(BF16)
        acc = acc + jnp.dot(act, wdn_ref[cols, :], preferred_element_type=F32)
    if final:
        acc = _rms(acc, gf_ref[...])
    o_ref[...] = acc


def _mlp(h2d, g, wup, wdn, gf, final, tm=1024, chunk=1024):
    T = h2d.shape[0]
    const = lambda a: pl.BlockSpec(a.shape, lambda m: (0,) * a.ndim, pipeline_mode=pl.Buffered(1))
    return pl.pallas_call(
        functools.partial(_mlp_kernel, final=final, chunk=chunk),
        grid=(T // tm,),
        in_specs=[pl.BlockSpec((tm, D_MODEL), lambda m: (m, 0)),
                  const(g), const(wup), const(wdn), const(gf)],
        out_specs=pl.BlockSpec((tm, D_MODEL), lambda m: (m, 0)),
        out_shape=jax.ShapeDtypeStruct((T, D_MODEL), F32),
        compiler_params=_cparams("parallel"),
        name="mlp",
    )(h2d, g, wup, wdn, gf)


def kernel(x, norm_mix, w_in, s5_lam_re, s5_lam_im, s5_log_dt, s5_b_re, s5_b_im, s5_c_re, s5_c_im, s5_d, s5_w_glu, pool_w, pool_scale, w_branch_s5, w_branch_pool, w_branch_attn, w_out, norm_mlp, w_up, w_down, norm_final):
    bsz, seq, _ = x.shape
    depth = w_in.shape[0]
    bf = lambda t: t.astype(BF16)
    row = lambda t: t.astype(F32).reshape(1, -1)
    h = x.astype(F32).reshape(bsz * seq, D_MODEL)
    s5_w = jax.vmap(_s5_params)(s5_lam_re, s5_lam_im, s5_log_dt, s5_b_re, s5_b_im, s5_c_re, s5_c_im)
    biases = [_attn_bias(dil) for dil in DILATIONS]
    for l in range(depth):
        us5_tm, u_pool, q, k, v, gates = _inproj(h, row(norm_mix[l]), bf(w_in[l]), bsz, seq)
        y_dirs = _s5_scan(us5_tm, *(w[l] for w in s5_w), bsz, seq)
        y_pool = _pool(u_pool, bf(pool_w[l]), row(pool_scale[l]), bsz, seq)
        outs, stats = zip(*[_attn_pattern(q[i], k[i], v[i], biases[i], dil, bsz, seq)
                            for i, dil in enumerate(DILATIONS)])
        h = _merge(y_dirs, us5_tm, row(s5_d[l]), bf(s5_w_glu[l]), y_pool, outs, stats, gates, h,
                   bf(w_branch_s5[l]), bf(w_branch_pool[l]), bf(w_branch_attn[l]), bf(w_out[l]),
                   bsz, seq)
        h = _mlp(h, row(norm_mlp[l]), bf(w_up[l]), bf(w_down[l]), row(norm_final), l == depth - 1)
    return h.reshape(bsz, seq, D_MODEL).astype(x.dtype)
```

```python
import functools
import math

import jax
import jax.numpy as jnp
from jax import lax
from jax.experimental import pallas as pl
from jax.experimental.pallas import tpu as pltpu

F32 = jnp.float32
BF16 = jnp.bfloat16

D_MODEL = 1024
S5_WIDTH = 512
S5_GROUP_SIZE = 16
S5_STATE = 64
POOL_WIDTH = 512
POOL_WINDOWS = (2, 4, 8, 16)
POOL_GROUP_SIZE = 128
ATTN_HEADS = 16
HEAD_DIM = 64
ATTN_WIDTH = 1024
ATTN_PATTERNS = ((128, 1), (512, 4), (2048, 16))
D_FF = 4096
NORM_EPS = 1e-6
NEG_INF = -1e30

LANES = 128
SUBLANES = 8
S5_UNIT_GROUPS = LANES // S5_GROUP_SIZE
S5_UNITS = S5_WIDTH // LANES
S5_UNIT_STATE = S5_UNIT_GROUPS * S5_STATE
QBLK = 128
KWIN = 2 * QBLK
ATTN_CHUNK = 256
ATTN_NCHUNK = ATTN_WIDTH // ATTN_CHUNK
VMEM_LIMIT = 56 * 1024 * 1024


def _cparams(*sem):
    return pltpu.CompilerParams(dimension_semantics=sem, vmem_limit_bytes=VMEM_LIMIT)


def _rms(x, g):
    ms = jnp.mean(x * x, axis=-1, keepdims=True)
    return x * lax.rsqrt(ms + NORM_EPS) * g


def _sigmoid(x):
    return 0.5 + 0.5 * jnp.tanh(0.5 * x)


DILATIONS = tuple(dil for _, dil in ATTN_PATTERNS)
LOG2E = math.log2(math.e)
Q_SCALE = HEAD_DIM ** -0.5 * LOG2E


def _inproj_kernel(x_ref, g_ref, w_ref, us5_ref, upool_ref, *rest, tm):
    nd = len(DILATIONS)
    qkv_refs = [rest[nd * i:nd * (i + 1)] for i in range(3)]
    gate_ref = rest[3 * nd]
    lay_refs = dict(zip(DILATIONS[:-1], rest[3 * nd + 1:]))
    xn = _rms(x_ref[...], g_ref[...]).astype(BF16)
    proj = lambda c: jnp.dot(xn, w_ref[:, D_MODEL * c:D_MODEL * (c + 1)], preferred_element_type=F32)

    for c in range(3):
        gate_ref[:, D_MODEL * c:D_MODEL * (c + 1)] = _sigmoid(proj(4 + c)).astype(BF16)

    for i, refs in enumerate(qkv_refs):
        acc = proj(1 + i)
        if i == 0:
            acc = acc * Q_SCALE
        refs[0][0] = acc.astype(BF16)
        for j in range(ATTN_WIDTH // LANES):
            lay_refs[1][j] = acc[:, LANES * j:LANES * (j + 1)]
        for base, dil, ref in zip(DILATIONS[:-1], DILATIONS[1:], refs[1:]):
            ratio, rows = dil // base, tm // dil
            for c in range(base):
                for a in range(ratio):
                    r = c + base * a
                    for j in range(ATTN_WIDTH // LANES):
                        piece = lay_refs[base][j, pl.ds(c * (tm // base) + a, rows, stride=ratio), :]
                        ref[r, :, LANES * j:LANES * (j + 1)] = piece.astype(BF16)
                        if dil in lay_refs:
                            lay_refs[dil][j, r * rows:(r + 1) * rows, :] = piece

    acc = proj(0)
    us5_ref[...] = acc[:, :S5_WIDTH].astype(BF16)
    upool_ref[...] = acc[:, S5_WIDTH:]


def _inproj(h2d, g, w_bf16, bsz, seq, tm=512):
    T = bsz * seq
    tps = seq // tm
    row = lambda m: (m, 0)
    const = lambda a: pl.BlockSpec(a.shape, lambda m: (0,) * a.ndim, pipeline_mode=pl.Buffered(1))
    lay_specs = [pl.BlockSpec((None, dil, tm // dil, ATTN_WIDTH), lambda m: (m // tps, 0, m % tps, 0))
                 for dil in DILATIONS]
    lay_shapes = [jax.ShapeDtypeStruct((bsz, dil, seq // dil, ATTN_WIDTH), BF16) for dil in DILATIONS]
    nd = len(DILATIONS)
    outs = pl.pallas_call(
        functools.partial(_inproj_kernel, tm=tm),
        grid=(T // tm,),
        in_specs=[pl.BlockSpec((tm, D_MODEL), row), const(g), const(w_bf16)],
        out_specs=[
            pl.BlockSpec((tm, S5_WIDTH), lambda m: (m % tps, m // tps)),
            pl.BlockSpec((tm, POOL_WIDTH), row),
            *(lay_specs * 3),
            pl.BlockSpec((tm, 3 * D_MODEL), row),
        ],
        out_shape=[
            jax.ShapeDtypeStruct((seq, bsz * S5_WIDTH), BF16),
            jax.ShapeDtypeStruct((T, POOL_WIDTH), F32),
            *(lay_shapes * 3),
            jax.ShapeDtypeStruct((T, 3 * D_MODEL), BF16),
        ],
        scratch_shapes=[pltpu.VMEM((ATTN_WIDTH // LANES, tm, LANES), F32)] * (nd - 1),
        compiler_params=_cparams("parallel"),
        name="inproj",
    )(h2d, g, w_bf16)
    q, k, v = (outs[2 + nd * i:2 + nd * (i + 1)] for i in range(3))
    return outs[0], outs[1], q, k, v, outs[2 + 3 * nd]


def _s5_kernel(u_ref, bu_ref, cu_ref, cb_ref, a_ref, y_ref, st_ref, tmp_ref, xe_ref, xo_ref,
               p_ref, ypar_ref, ynat_ref, *sig_refs, tt, bsz):
    d = pl.program_id(0)
    i = pl.program_id(1)
    hs = S5_UNIT_STATE
    npair = tt // 2
    prows = npair * bsz
    nu = S5_UNITS

    @pl.when(i == 0)
    def _():
        st_ref[...] = jnp.zeros_like(st_ref)

    for b in range(bsz):
        for j in range(nu):
            lanes = slice(S5_WIDTH * b + LANES * j, S5_WIDTH * b + LANES * (j + 1))
            tmp_ref[b * nu + j] = u_ref[:, lanes].astype(F32)
            xe_ref[j, pl.ds(b, npair, stride=bsz), :] = tmp_ref[b * nu + j, pl.ds(0, npair, stride=2), :]
            xo_ref[j, pl.ds(b, npair, stride=bsz), :] = tmp_ref[b * nu + j, pl.ds(1, npair, stride=2), :]

    lead = pl.multiple_of((1 - d) * bsz, bsz)
    carry_rows = pl.ds(pl.multiple_of(d * prows, bsz), bsz)
    x2 = []
    for j in range(nu):
        x2.append(jnp.concatenate([xe_ref[j], xo_ref[j]], axis=1).astype(BF16))
        sig_refs[j][pl.ds(lead, prows), :] = jnp.dot(x2[j], bu_ref[j], preferred_element_type=F32)
        sig_refs[j][carry_rows, :] = st_ref[:, 2 * hs * j:2 * hs * (j + 1)]

    for j0 in range(0, nu, 2):
        units = (j0, j0 + 1)
        coef = [(jnp.broadcast_to(a_ref[0:1, hs * j:hs * (j + 1)], (bsz, hs)),
                 jnp.broadcast_to(a_ref[1:2, hs * j:hs * (j + 1)], (bsz, hs))) for j in units]
        init = tuple((st_ref[:, 2 * hs * j:2 * hs * j + hs], st_ref[:, 2 * hs * j + hs:2 * hs * (j + 1)])
                     for j in units)

        def step(s, carry, units=units, coef=coef):
            m = jnp.where(d == 0, s, npair - 1 - s)
            rows = pl.ds(pl.multiple_of(m * bsz + lead, bsz), bsz)
            out = []
            for j, (ar, ai), (sr, si) in zip(units, coef, carry):
                sig = sig_refs[j]
                nr = ar * sr - ai * si + sig[rows, :hs]
                ni = ar * si + ai * sr + sig[rows, hs:]
                sig[rows, :hs] = nr
                sig[rows, hs:] = ni
                out.append((nr, ni))
            return tuple(out)

        final = lax.fori_loop(0, npair, step, init, unroll=4)
        for j, (sr, si) in zip(units, final):
            st_ref[:, 2 * hs * j:2 * hs * j + hs] = sr
            st_ref[:, 2 * hs * j + hs:2 * hs * (j + 1)] = si

    for j in range(nu):
        p_ref[j] = jnp.dot(sig_refs[j][...].astype(BF16), cu_ref[j], preferred_element_type=F32)
        ypar_ref[0, j] = p_ref[j, pl.ds(lead, prows), :LANES]
        ypar_ref[1, j] = (p_ref[j, pl.ds(bsz - lead, prows), LANES:]
                          + jnp.dot(x2[j], cb_ref[j], preferred_element_type=F32))

    for b in range(bsz):
        for j in range(nu):
            lanes = slice(S5_WIDTH * b + LANES * j, S5_WIDTH * b + LANES * (j + 1))
            k = b * nu + j
            ynat_ref[k, pl.ds(1 - d, npair, stride=2), :] = ypar_ref[0, j, pl.ds(b, npair, stride=bsz), :]
            ynat_ref[k, pl.ds(d, npair, stride=2), :] = ypar_ref[1, j, pl.ds(b, npair, stride=bsz), :]
            y_ref[:, lanes] = ynat_ref[k].astype(BF16)


def _s5_scan(u_tm, bu, cu, cb, a2, bsz, seq, tt=128):
    assert bsz == SUBLANES and tt % 2 == 0
    n_t = seq // tt
    tile = lambda d, i: i + d * (n_t - 1 - 2 * i)
    uw = 2 * S5_UNIT_STATE
    prows = tt // 2 * bsz
    per_dir = lambda a: pl.BlockSpec((None,) + a.shape[1:], lambda d, i: (d,) + (0,) * (a.ndim - 1))
    return pl.pallas_call(
        functools.partial(_s5_kernel, tt=tt, bsz=bsz),
        grid=(2, n_t),
        in_specs=[pl.BlockSpec((tt, bsz * S5_WIDTH), lambda d, i: (tile(d, i), 0)),
                  per_dir(bu), per_dir(cu), per_dir(cb), per_dir(a2)],
        out_specs=pl.BlockSpec((None, tt, bsz * S5_WIDTH), lambda d, i: (d, tile(d, i), 0)),
        out_shape=jax.ShapeDtypeStruct((2, seq, bsz * S5_WIDTH), BF16),
        scratch_shapes=[pltpu.VMEM((bsz, S5_UNITS * uw), F32),
                        pltpu.VMEM((bsz * S5_UNITS, tt, LANES), F32),
                        pltpu.VMEM((S5_UNITS, prows, LANES), F32),
                        pltpu.VMEM((S5_UNITS, prows, LANES), F32),
                        pltpu.VMEM((S5_UNITS, prows + bsz, 2 * LANES), F32),
                        pltpu.VMEM((2, S5_UNITS, prows, LANES), F32),
                        pltpu.VMEM((bsz * S5_UNITS, tt, LANES), F32)]
        + [pltpu.VMEM((prows + bsz, uw), F32)] * S5_UNITS,
        compiler_params=_cparams("arbitrary", "arbitrary"),
        name="s5_scan",
    )(u_tm, bu, cu, cb, a2)


def _s5_params(lam_re, lam_im, log_dt, b_re, b_im, c_re, c_im):
    f = lambda t: t.astype(F32)
    lam_re, lam_im, b_re, b_im, c_re, c_im = map(f, (lam_re, lam_im, b_re, b_im, c_re, c_im))
    dt = jnp.exp(f(log_dt))[:, :, None]
    mag = jnp.exp(lam_re * dt)
    a_re = mag * jnp.cos(lam_im * dt)
    a_im = mag * jnp.sin(lam_im * dt)
    den = lam_re * lam_re + lam_im * lam_im
    f_re = ((a_re - 1.0) * lam_re + a_im * lam_im) / den
    f_im = (a_im * lam_re - (a_re - 1.0) * lam_im) / den
    bb_re = f_re[..., None] * b_re - f_im[..., None] * b_im
    bb_im = f_re[..., None] * b_im + f_im[..., None] * b_re
    ug, nu = S5_UNIT_GROUPS, S5_UNITS
    eye = jnp.eye(ug, dtype=F32)

    def pack_b(t):
        t = t.reshape(2, nu, ug, S5_STATE, S5_GROUP_SIZE)
        t = jnp.einsum('dugnp,gh->dugphn', t, eye)
        return t.reshape(2, nu, LANES, S5_UNIT_STATE)

    def pack_c(t):
        t = t.reshape(2, nu, ug, S5_GROUP_SIZE, S5_STATE)
        t = jnp.einsum('dugpn,gh->duhngp', t, eye)
        return t.reshape(2, nu, S5_UNIT_STATE, LANES)

    def pack_pp(t):
        t = t.reshape(2, nu, ug, S5_GROUP_SIZE, S5_GROUP_SIZE)
        t = jnp.einsum('dugpq,gh->dugqhp', t, eye)
        return t.reshape(2, nu, LANES, LANES)

    ab_re = a_re[..., None] * bb_re - a_im[..., None] * bb_im
    ab_im = a_re[..., None] * bb_im + a_im[..., None] * bb_re
    b_plain = jnp.concatenate([pack_b(bb_re), pack_b(bb_im)], axis=-1)
    b_first = jnp.concatenate([pack_b(ab_re), pack_b(ab_im)], axis=-1)
    bu = jnp.stack([jnp.concatenate([b_first[0], b_plain[0]], axis=-2),
                    jnp.concatenate([b_plain[1], b_first[1]], axis=-2)]).astype(BF16)

    ca_re = c_re * a_re[:, :, None, :] - c_im * a_im[:, :, None, :]
    ca_im = c_re * a_im[:, :, None, :] + c_im * a_re[:, :, None, :]
    c_plain = jnp.concatenate([pack_c(c_re), -pack_c(c_im)], axis=-2)
    c_a = jnp.concatenate([pack_c(ca_re), -pack_c(ca_im)], axis=-2)
    cu = jnp.concatenate([c_plain, c_a], axis=-1).astype(BF16)

    cb_pq = (jnp.einsum('dgpn,dgnq->dgpq', c_re, bb_re) - jnp.einsum('dgpn,dgnq->dgpq', c_im, bb_im))
    cb_mat = pack_pp(cb_pq)
    zero = jnp.zeros_like(cb_mat[0])
    cb = jnp.stack([jnp.concatenate([cb_mat[0], zero], axis=-2),
                    jnp.concatenate([zero, cb_mat[1]], axis=-2)]).astype(BF16)

    a2_re = a_re * a_re - a_im * a_im
    a2_im = 2.0 * a_re * a_im
    a2 = jnp.stack([a2_re.reshape(2, -1), a2_im.reshape(2, -1)], axis=1)
    return bu, cu, cb, a2


POOL_HALO = SUBLANES
POOL_ROWS = 256


def _pool_mix(xs_ref, w_ref, s_ref, t0, seq, tm):
    n = POOL_ROWS + 2 * POOL_HALO
    out = []
    for c in range(tm // POOL_ROWS):
        t = t0 + POOL_ROWS * c + lax.broadcasted_iota(jnp.int32, (POOL_ROWS, 1), 0)
        groups = []
        for g, win in enumerate(POOL_WINDOWS):
            lanes = slice(POOL_GROUP_SIZE * g, POOL_GROUP_SIZE * (g + 1))
            x = xs_ref[POOL_ROWS * c:POOL_ROWS * c + n, lanes]
            s = x
            k = 1
            while k < win:
                s = s + pltpu.roll(s, k, axis=0)
                k *= 2
            left = win // 2
            right = win - 1 - left
            if right:
                s = pltpu.roll(s, n - right, axis=0)
            hi = jnp.minimum(t + right + 1, seq)
            lo = jnp.maximum(t - left, 0)
            mean = s[POOL_HALO:POOL_HALO + POOL_ROWS] / (hi - lo).astype(F32)
            pooled = mean - x[POOL_HALO:POOL_HALO + POOL_ROWS]
            mixed = jnp.dot(pooled.astype(BF16), w_ref[g], preferred_element_type=F32)
            groups.append((mixed * s_ref[:, lanes]).astype(BF16))
        out.append(jnp.concatenate(groups, axis=1))
    return jnp.concatenate(out, axis=0)


def _stat_lane(head):
    return head + HEAD_DIM * (1 - head % 2)


STAT_MAX_SHIFT = 8


def _out_pitch(dil):
    return dil + SUBLANES if dil % (2 * SUBLANES) == 0 else dil


def _attn_kernel(q_ref, k_ref, v_ref, bias_ref, o_ref, stat_ref, *scr, dil, m_rows, pairs, unroll):
    hc = pl.program_id(1)
    nqb = m_rows // QBLK
    pitch = _out_pitch(dil)
    stat = stat_ref if dil == 1 else scr[1]
    stat[...] = jnp.ones_like(stat)

    low = lax.broadcasted_iota(jnp.int32, (QBLK, LANES), 1) < HEAD_DIM

    def qblock(it, carry):
        r = it // nqb
        m0 = pl.multiple_of((it % nqb) * QBLK, QBLK)
        ks = pl.multiple_of(jnp.clip(m0 - QBLK // 2, 0, m_rows - KWIN), QBLK // 2)
        var = (m0 - ks) // (QBLK // 2)
        qrows = pl.ds(m0, QBLK)
        krows = pl.ds(ks, KWIN)
        srows = pl.ds(pl.multiple_of(r * m_rows + m0, QBLK), QBLK)
        for j in range(pairs):
            lanes = slice(LANES * j, LANES * (j + 1))
            qp = q_ref[r, qrows, lanes]
            zero = jnp.zeros_like(qp)
            q2 = jnp.concatenate([jnp.where(low, qp, zero), jnp.where(low, zero, qp)], axis=0)
            s2 = lax.dot_general(q2, k_ref[r, krows, lanes], (((1,), (1,)), ((), ())),
                                 preferred_element_type=F32)
            vp = v_ref[r, krows, lanes]
            halves = []
            for e in range(2):
                s = s2[QBLK * e:QBLK * (e + 1)] + bias_ref[var, 2 * (hc * pairs + j) + e]
                mx = jnp.max(s, axis=-1, keepdims=True)
                p = jnp.exp2(s - mx)
                halves.append(jnp.dot(p.astype(BF16), vp, preferred_element_type=F32))
                sl = _stat_lane(2 * j + e)
                stat[srows, sl:sl + 1] = jnp.sum(p, axis=-1, keepdims=True)
                stat[srows, sl + STAT_MAX_SHIFT:sl + STAT_MAX_SHIFT + 1] = mx
            o_pair = jnp.where(low, halves[0], halves[1])
            if dil == 1:
                o_ref[qrows, lanes] = o_pair.astype(BF16)
            else:
                scr[0][j, pl.ds(r + pitch * m0, QBLK, stride=pitch), :] = o_pair
        return carry

    lax.fori_loop(0, dil * nqb, qblock, 0, unroll=unroll)
    if dil > 1:
        for j in range(pairs):
            lanes = slice(LANES * j, LANES * (j + 1))
            if pitch == dil:
                o_ref[:, lanes] = scr[0][j].astype(BF16)
            else:
                for m in range(m_rows):
                    o_ref[dil * m:dil * (m + 1), lanes] = scr[0][j, pitch * m:pitch * m + dil, :].astype(BF16)
        for r in range(dil):
            rows = slice(r * m_rows, (r + 1) * m_rows)
            stat_ref[pl.ds(r, m_rows, stride=dil), :] = stat[rows, :]


def _attn_bias(dil):
    slopes = jnp.exp2(-8.0 * jnp.arange(1, ATTN_HEADS + 1, dtype=F32) / ATTN_HEADS)
    a = jnp.arange(QBLK)[:, None]
    c = jnp.arange(KWIN)[None, :]
    off = (jnp.arange(3) * (QBLK // 2))[:, None, None]
    rel = c[None] - off - a[None]
    dist = (dil * jnp.abs(rel)).astype(F32)
    bias = -LOG2E * slopes[None, :, None, None] * dist[:, None]
    return jnp.where((jnp.abs(rel) <= QBLK // 2)[:, None], bias, NEG_INF)


def _attn_pattern(q, k, v, bias, dil, bsz, seq, hw=ATTN_CHUNK, unroll=32):
    m_rows = seq // dil
    nhc = ATTN_WIDTH // hw
    pairs = hw // LANES
    blk = pl.BlockSpec((None, dil, m_rows, hw), lambda b, h: (b, 0, 0, h))
    stat = pl.BlockSpec((seq, LANES), lambda b, h: (b, h))
    regroup = [pltpu.VMEM((pairs, m_rows * _out_pitch(dil), LANES), F32), pltpu.VMEM((seq, LANES), F32)]
    return pl.pallas_call(
        functools.partial(_attn_kernel, dil=dil, m_rows=m_rows, pairs=pairs, unroll=unroll),
        grid=(bsz, nhc),
        in_specs=[blk, blk, blk,
                  pl.BlockSpec(bias.shape, lambda b, h: (0, 0, 0, 0), pipeline_mode=pl.Buffered(1))],
        out_specs=[pl.BlockSpec((seq, hw), lambda b, h: (b, h)), stat],
        out_shape=[jax.ShapeDtypeStruct((bsz * seq, ATTN_WIDTH), BF16),
                   jax.ShapeDtypeStruct((bsz * seq, nhc * LANES), F32)],
        scratch_shapes=regroup if dil > 1 else [],
        compiler_params=_cparams("parallel", "parallel"),
        name=f"attn_d{dil}",
    )(q, k, v, bias)


def _merge_kernel(yf_ref, yb_ref, us5_ref, dsk_ref, wglu_ref,
                  up_ref, uprev_ref, unext_ref, pw_ref, ps_ref,
                  o1_ref, o2_ref, o3_ref, s1_ref, s2_ref, s3_ref, exp_ref,
                  gate_ref, h_ref, wbs_ref, wbp_ref, wba_ref, wout_ref, out_ref, xs_ref, *, seq):
    dot = functools.partial(jnp.dot, preferred_element_type=F32)
    tm = h_ref.shape[0]
    i = pl.program_id(1)
    y = (yf_ref[...].astype(F32) + yb_ref[...].astype(F32)
         + dsk_ref[...] * us5_ref[...].astype(F32))
    y = jax.nn.gelu(y)
    y = y * _sigmoid(dot(y.astype(BF16), wglu_ref[...]))
    br_s5 = dot(y.astype(BF16), wbs_ref[...])

    xs_ref[0:POOL_HALO, :] = jnp.where(i > 0, uprev_ref[...], 0.0)
    xs_ref[POOL_HALO:POOL_HALO + tm, :] = up_ref[...]
    xs_ref[POOL_HALO + tm:, :] = jnp.where(i < pl.num_programs(1) - 1, unext_ref[...], 0.0)
    br_pool = dot(_pool_mix(xs_ref, pw_ref, ps_ref, i * tm, seq, tm), wbp_ref[...])

    lane = lax.broadcasted_iota(jnp.int32, (h_ref.shape[0], LANES), 1)
    is_stat = functools.reduce(jnp.logical_or,
                               [lane == _stat_lane(hd) for hd in range(ATTN_CHUNK // HEAD_DIM)])
    att = []
    for c in range(ATTN_NCHUNK):
        st, ch = slice(LANES * c, LANES * (c + 1)), slice(ATTN_CHUNK * c, ATTN_CHUNK * (c + 1))
        d1, d2, d3 = s1_ref[:, st], s2_ref[:, st], s3_ref[:, st]
        m1, m2, m3 = (pltpu.roll(s, LANES - STAT_MAX_SHIFT, axis=1) for s in (d1, d2, d3))
        mx = jnp.maximum(jnp.maximum(m1, m2), m3)
        w1, w2, w3 = jnp.exp2(m1 - mx), jnp.exp2(m2 - mx), jnp.exp2(m3 - mx)
        inv = 1.0 / (w1 * d1 + w2 * d2 + w3 * d3)
        expand = lambda w: dot(jnp.where(is_stat, w * inv, 0.0).astype(BF16), exp_ref[...])
        att.append(expand(w1) * o1_ref[:, ch].astype(F32) + expand(w2) * o2_ref[:, ch].astype(F32)
                   + expand(w3) * o3_ref[:, ch].astype(F32))
    br_attn = dot(jnp.concatenate(att, axis=1).astype(BF16), wba_ref[...])

    merged = (gate_ref[:, 0:D_MODEL].astype(F32) * br_s5
              + gate_ref[:, D_MODEL:2 * D_MODEL].astype(F32) * br_pool
              + gate_ref[:, 2 * D_MODEL:3 * D_MODEL].astype(F32) * br_attn)
    out_ref[...] = h_ref[...] + dot(merged.astype(BF16), wout_ref[...])


def _merge(y_dirs, us5_tm, dsk, wglu, u_pool, pool_w, pool_scale, outs, stats, gates, h2d,
           wbs, wbp, wba, wout, bsz, seq, tm=512):
    T = bsz * seq
    tps = seq // tm
    hpt = tm // POOL_HALO
    row = lambda w: pl.BlockSpec((tm, w), lambda b, i: (b * tps + i, 0))
    full = lambda a: pl.BlockSpec(a.shape, lambda b, i: (0,) * a.ndim)
    ydir = lambda d: pl.BlockSpec((None, tm, S5_WIDTH), lambda b, i: (d, i, b))
    halo_prev = pl.BlockSpec((POOL_HALO, POOL_WIDTH),
                             lambda b, i: (jnp.maximum((b * tps + i) * hpt - 1, 0), 0))
    halo_next = pl.BlockSpec((POOL_HALO, POOL_WIDTH),
                             lambda b, i: (jnp.minimum((b * tps + i + 1) * hpt, T // POOL_HALO - 1), 0))
    local_head = jnp.arange(ATTN_CHUNK)[None, :] // HEAD_DIM
    expand = (jnp.arange(LANES)[:, None] == _stat_lane(local_head)).astype(BF16)
    return pl.pallas_call(
        functools.partial(_merge_kernel, seq=seq),
        grid=(bsz, tps),
        in_specs=[ydir(0), ydir(1),
                  pl.BlockSpec((tm, S5_WIDTH), lambda b, i: (i, b)),
                  full(dsk), full(wglu),
                  row(POOL_WIDTH), halo_prev, halo_next, full(pool_w), full(pool_scale),
                  row(ATTN_WIDTH), row(ATTN_WIDTH), row(ATTN_WIDTH),
                  *([row(ATTN_NCHUNK * LANES)] * 3), full(expand),
                  row(3 * D_MODEL), row(D_MODEL),
                  full(wbs), full(wbp), full(wba), full(wout)],
        out_specs=row(D_MODEL),
        out_shape=jax.ShapeDtypeStruct((T, D_MODEL), F32),
        scratch_shapes=[pltpu.VMEM((tm + 2 * POOL_HALO, POOL_WIDTH), F32)],
        compiler_params=_cparams("parallel", "parallel"),
        name="merge",
    )(y_dirs, y_dirs, us5_tm, dsk, wglu, u_pool, u_pool, u_pool, pool_w, pool_scale,
      *outs, *stats, expand, gates, h2d, wbs, wbp, wba, wout)


def _mlp_kernel(h_ref, g_ref, wup_ref, wdn_ref, gf_ref, o_ref, *, final, chunk):
    h = h_ref[...]
    hn = _rms(h, g_ref[...]).astype(BF16)
    acc = h
    for c in range(D_FF // chunk):
        cols = slice(chunk * c, chunk * (c + 1))
        up = jnp.dot(hn, wup_ref[:, cols], preferred_element_type=F32)
        act = jnp.square(jnp.maximum(up, 0.0)).astype(BF16)
        acc = acc + jnp.dot(act, wdn_ref[cols, :], preferred_element_type=F32)
    if final:
        acc = _rms(acc, gf_ref[...])
    o_ref[...] = acc


def _mlp(h2d, g, wup, wdn, gf, final, tm=1024, chunk=1024):
    T = h2d.shape[0]
    const = lambda a: pl.BlockSpec(a.shape, lambda m: (0,) * a.ndim, pipeline_mode=pl.Buffered(1))
    return pl.pallas_call(
        functools.partial(_mlp_kernel, final=final, chunk=chunk),
        grid=(T // tm,),
        in_specs=[pl.BlockSpec((tm, D_MODEL), lambda m: (m, 0)),
                  const(g), const(wup), const(wdn), const(gf)],
        out_specs=pl.BlockSpec((tm, D_MODEL), lambda m: (m, 0)),
        out_shape=jax.ShapeDtypeStruct((T, D_MODEL), F32),
        compiler_params=_cparams("parallel"),
        name="mlp",
    )(h2d, g, wup, wdn, gf)


def kernel(x, norm_mix, w_in, s5_lam_re, s5_lam_im, s5_log_dt, s5_b_re, s5_b_im, s5_c_re, s5_c_im, s5_d, s5_w_glu, pool_w, pool_scale, w_branch_s5, w_branch_pool, w_branch_attn, w_out, norm_mlp, w_up, w_down, norm_final):
    bsz, seq, _ = x.shape
    depth = w_in.shape[0]
    bf = lambda t: t.astype(BF16)
    row = lambda t: t.astype(F32).reshape(1, -1)
    h = x.astype(F32).reshape(bsz * seq, D_MODEL)
    s5_w = jax.vmap(_s5_params)(s5_lam_re, s5_lam_im, s5_log_dt, s5_b_re, s5_b_im, s5_c_re, s5_c_im)
    biases = [_attn_bias(dil) for dil in DILATIONS]
    for l in range(depth):
        us5_tm, u_pool, q, k, v, gates = _inproj(h, row(norm_mix[l]), bf(w_in[l]), bsz, seq)
        y_dirs = _s5_scan(us5_tm, *(w[l] for w in s5_w), bsz, seq)
        outs, stats = zip(*[_attn_pattern(q[i], k[i], v[i], biases[i], dil, bsz, seq)
                            for i, dil in enumerate(DILATIONS)])
        h = _merge(y_dirs, us5_tm, row(s5_d[l]), bf(s5_w_glu[l]),
                   u_pool, bf(pool_w[l]), row(pool_scale[l]), outs, stats, gates, h,
                   bf(w_branch_s5[l]), bf(w_branch_pool[l]), bf(w_branch_attn[l]), bf(w_out[l]),
                   bsz, seq)
        h = _mlp(h, row(norm_mlp[l]), bf(w_up[l]), bf(w_down[l]), row(norm_final), l == depth - 1)
    return h.reshape(bsz, seq, D_MODEL).astype(x.dtype)
```

```python
import functools
import math

import jax
import jax.numpy as jnp
from jax import lax
from jax.experimental import pallas as pl
from jax.experimental.pallas import tpu as pltpu

F32 = jnp.float32
BF16 = jnp.bfloat16

D_MODEL = 1024
S5_WIDTH = 512
S5_GROUP_SIZE = 16
S5_STATE = 64
POOL_WIDTH = 512
POOL_WINDOWS = (2, 4, 8, 16)
POOL_GROUP_SIZE = 128
ATTN_HEADS = 16
HEAD_DIM = 64
ATTN_WIDTH = 1024
ATTN_PATTERNS = ((128, 1), (512, 4), (2048, 16))
D_FF = 4096
NORM_EPS = 1e-6
NEG_INF = -1e30

LANES = 128
SUBLANES = 8
S5_UNIT_GROUPS = LANES // S5_GROUP_SIZE
S5_UNITS = S5_WIDTH // LANES
S5_UNIT_STATE = S5_UNIT_GROUPS * S5_STATE
QBLK = 128
KWIN = 2 * QBLK
ATTN_CHUNK = 256
ATTN_NCHUNK = ATTN_WIDTH // ATTN_CHUNK
VMEM_LIMIT = 56 * 1024 * 1024


def _cparams(*sem):
    return pltpu.CompilerParams(dimension_semantics=sem, vmem_limit_bytes=VMEM_LIMIT)


def _rms(x, g):
    ms = jnp.mean(x * x, axis=-1, keepdims=True)
    return x * lax.rsqrt(ms + NORM_EPS) * g


def _layer_spec(a, layer, **kw):
    rest = (0,) * (a.ndim - 1)
    return pl.BlockSpec((None,) + a.shape[1:], lambda *_: (layer,) + rest, **kw)


def _sigmoid(x):
    return 0.5 + 0.5 * jnp.tanh(0.5 * x)


DILATIONS = tuple(dil for _, dil in ATTN_PATTERNS)
LOG2E = math.log2(math.e)
Q_SCALE = HEAD_DIM ** -0.5 * LOG2E


def _inproj_kernel(x_ref, g_ref, w_ref, us5_ref, upool_ref, *rest, tm):
    nd = len(DILATIONS)
    qkv_refs = [rest[nd * i:nd * (i + 1)] for i in range(3)]
    gate_ref = rest[3 * nd]
    lay_refs = dict(zip(DILATIONS[:-1], rest[3 * nd + 1:]))
    xn = _rms(x_ref[...], g_ref[...]).astype(BF16)
    proj = lambda c: jnp.dot(xn, w_ref[:, D_MODEL * c:D_MODEL * (c + 1)], preferred_element_type=F32)

    for c in range(3):
        gate_ref[:, D_MODEL * c:D_MODEL * (c + 1)] = _sigmoid(proj(4 + c)).astype(BF16)

    for i, refs in enumerate(qkv_refs):
        acc = proj(1 + i)
        if i == 0:
            acc = acc * Q_SCALE
        refs[0][0] = acc.astype(BF16)
        for j in range(ATTN_WIDTH // LANES):
            lay_refs[1][j] = acc[:, LANES * j:LANES * (j + 1)]
        for base, dil, ref in zip(DILATIONS[:-1], DILATIONS[1:], refs[1:]):
            ratio, rows = dil // base, tm // dil
            for c in range(base):
                for a in range(ratio):
                    r = c + base * a
                    for j in range(ATTN_WIDTH // LANES):
                        piece = lay_refs[base][j, pl.ds(c * (tm // base) + a, rows, stride=ratio), :]
                        ref[r, :, LANES * j:LANES * (j + 1)] = piece.astype(BF16)
                        if dil in lay_refs:
                            lay_refs[dil][j, r * rows:(r + 1) * rows, :] = piece

    acc = proj(0)
    us5_ref[...] = acc[:, :S5_WIDTH].astype(BF16)
    upool_ref[...] = acc[:, S5_WIDTH:]


def _inproj(h2d, g, w_bf16, layer, bsz, seq, tm=512):
    T = bsz * seq
    tps = seq // tm
    row = lambda m: (m, 0)
    const = lambda a: _layer_spec(a, layer, pipeline_mode=pl.Buffered(1))
    lay_specs = [pl.BlockSpec((None, dil, tm // dil, ATTN_WIDTH), lambda m: (m // tps, 0, m % tps, 0))
                 for dil in DILATIONS]
    lay_shapes = [jax.ShapeDtypeStruct((bsz, dil, seq // dil, ATTN_WIDTH), BF16) for dil in DILATIONS]
    nd = len(DILATIONS)
    outs = pl.pallas_call(
        functools.partial(_inproj_kernel, tm=tm),
        grid=(T // tm,),
        in_specs=[pl.BlockSpec((tm, D_MODEL), row), const(g), const(w_bf16)],
        out_specs=[
            pl.BlockSpec((tm, S5_WIDTH), lambda m: (m % tps, m // tps)),
            pl.BlockSpec((tm, POOL_WIDTH), row),
            *(lay_specs * 3),
            pl.BlockSpec((tm, 3 * D_MODEL), row),
        ],
        out_shape=[
            jax.ShapeDtypeStruct((seq, bsz * S5_WIDTH), BF16),
            jax.ShapeDtypeStruct((T, POOL_WIDTH), F32),
            *(lay_shapes * 3),
            jax.ShapeDtypeStruct((T, 3 * D_MODEL), BF16),
        ],
        scratch_shapes=[pltpu.VMEM((ATTN_WIDTH // LANES, tm, LANES), F32)] * (nd - 1),
        compiler_params=_cparams("parallel"),
        name="inproj",
    )(h2d, g, w_bf16)
    q, k, v = (outs[2 + nd * i:2 + nd * (i + 1)] for i in range(3))
    return outs[0], outs[1], q, k, v, outs[2 + 3 * nd]


def _s5_kernel(u_ref, bu_ref, cu_ref, cb_ref, a_ref, y_ref, st_ref, tmp_ref, xe_ref, xo_ref,
               p_ref, ypar_ref, ynat_ref, *sig_refs, tt, bsz):
    d = pl.program_id(0)
    i = pl.program_id(1)
    hs = S5_UNIT_STATE
    npair = tt // 2
    prows = npair * bsz
    nu = S5_UNITS

    @pl.when(i == 0)
    def _():
        st_ref[...] = jnp.zeros_like(st_ref)

    for b in range(bsz):
        for j in range(nu):
            lanes = slice(S5_WIDTH * b + LANES * j, S5_WIDTH * b + LANES * (j + 1))
            tmp_ref[b * nu + j] = u_ref[:, lanes].astype(F32)
            xe_ref[j, pl.ds(b, npair, stride=bsz), :] = tmp_ref[b * nu + j, pl.ds(0, npair, stride=2), :]
            xo_ref[j, pl.ds(b, npair, stride=bsz), :] = tmp_ref[b * nu + j, pl.ds(1, npair, stride=2), :]

    lead = pl.multiple_of((1 - d) * bsz, bsz)
    carry_rows = pl.ds(pl.multiple_of(d * prows, bsz), bsz)
    x2 = []
    for j in range(nu):
        x2.append(jnp.concatenate([xe_ref[j], xo_ref[j]], axis=1).astype(BF16))
        sig_refs[j][pl.ds(lead, prows), :] = jnp.dot(x2[j], bu_ref[j], preferred_element_type=F32)
        sig_refs[j][carry_rows, :] = st_ref[:, 2 * hs * j:2 * hs * (j + 1)]

    for j0 in range(0, nu, 2):
        units = (j0, j0 + 1)
        coef = [(jnp.broadcast_to(a_ref[0:1, hs * j:hs * (j + 1)], (bsz, hs)),
                 jnp.broadcast_to(a_ref[1:2, hs * j:hs * (j + 1)], (bsz, hs))) for j in units]
        init = tuple((st_ref[:, 2 * hs * j:2 * hs * j + hs], st_ref[:, 2 * hs * j + hs:2 * hs * (j + 1)])
                     for j in units)

        def step(s, carry, units=units, coef=coef):
            m = jnp.where(d == 0, s, npair - 1 - s)
            rows = pl.ds(pl.multiple_of(m * bsz + lead, bsz), bsz)
            out = []
            for j, (ar, ai), (sr, si) in zip(units, coef, carry):
                sig = sig_refs[j]
                nr = ar * sr - ai * si + sig[rows, :hs]
                ni = ar * si + ai * sr + sig[rows, hs:]
                sig[rows, :hs] = nr
                sig[rows, hs:] = ni
                out.append((nr, ni))
            return tuple(out)

        final = lax.fori_loop(0, npair, step, init, unroll=4)
        for j, (sr, si) in zip(units, final):
            st_ref[:, 2 * hs * j:2 * hs * j + hs] = sr
            st_ref[:, 2 * hs * j + hs:2 * hs * (j + 1)] = si

    for j in range(nu):
        p_ref[j] = jnp.dot(sig_refs[j][...].astype(BF16), cu_ref[j], preferred_element_type=F32)
        ypar_ref[0, j] = p_ref[j, pl.ds(lead, prows), :LANES]
        ypar_ref[1, j] = (p_ref[j, pl.ds(bsz - lead, prows), LANES:]
                          + jnp.dot(x2[j], cb_ref[j], preferred_element_type=F32))

    for b in range(bsz):
        for j in range(nu):
            lanes = slice(S5_WIDTH * b + LANES * j, S5_WIDTH * b + LANES * (j + 1))
            k = b * nu + j
            ynat_ref[k, pl.ds(1 - d, npair, stride=2), :] = ypar_ref[0, j, pl.ds(b, npair, stride=bsz), :]
            ynat_ref[k, pl.ds(d, npair, stride=2), :] = ypar_ref[1, j, pl.ds(b, npair, stride=bsz), :]
            y_ref[:, lanes] = ynat_ref[k].astype(BF16)


def _s5_scan(u_tm, bu, cu, cb, a2, layer, bsz, seq, tt=128):
    assert bsz == SUBLANES and tt % 2 == 0
    n_t = seq // tt
    tile = lambda d, i: i + d * (n_t - 1 - 2 * i)
    uw = 2 * S5_UNIT_STATE
    prows = tt // 2 * bsz
    per_dir = lambda a: pl.BlockSpec((None, None) + a.shape[2:],
                                     lambda d, i: (layer, d) + (0,) * (a.ndim - 2))
    return pl.pallas_call(
        functools.partial(_s5_kernel, tt=tt, bsz=bsz),
        grid=(2, n_t),
        in_specs=[pl.BlockSpec((tt, bsz * S5_WIDTH), lambda d, i: (tile(d, i), 0)),
                  per_dir(bu), per_dir(cu), per_dir(cb), per_dir(a2)],
        out_specs=pl.BlockSpec((None, tt, bsz * S5_WIDTH), lambda d, i: (d, tile(d, i), 0)),
        out_shape=jax.ShapeDtypeStruct((2, seq, bsz * S5_WIDTH), BF16),
        scratch_shapes=[pltpu.VMEM((bsz, S5_UNITS * uw), F32),
                        pltpu.VMEM((bsz * S5_UNITS, tt, LANES), F32),
                        pltpu.VMEM((S5_UNITS, prows, LANES), F32),
                        pltpu.VMEM((S5_UNITS, prows, LANES), F32),
                        pltpu.VMEM((S5_UNITS, prows + bsz, 2 * LANES), F32),
                        pltpu.VMEM((2, S5_UNITS, prows, LANES), F32),
                        pltpu.VMEM((bsz * S5_UNITS, tt, LANES), F32)]
        + [pltpu.VMEM((prows + bsz, uw), F32)] * S5_UNITS,
        compiler_params=_cparams("arbitrary", "arbitrary"),
        name="s5_scan",
    )(u_tm, bu, cu, cb, a2)


def _s5_params(lam_re, lam_im, log_dt, b_re, b_im, c_re, c_im):
    f = lambda t: t.astype(F32)
    lam_re, lam_im, b_re, b_im, c_re, c_im = map(f, (lam_re, lam_im, b_re, b_im, c_re, c_im))
    dt = jnp.exp(f(log_dt))[:, :, None]
    mag = jnp.exp(lam_re * dt)
    a_re = mag * jnp.cos(lam_im * dt)
    a_im = mag * jnp.sin(lam_im * dt)
    den = lam_re * lam_re + lam_im * lam_im
    f_re = ((a_re - 1.0) * lam_re + a_im * lam_im) / den
    f_im = (a_im * lam_re - (a_re - 1.0) * lam_im) / den
    bb_re = f_re[..., None] * b_re - f_im[..., None] * b_im
    bb_im = f_re[..., None] * b_im + f_im[..., None] * b_re
    ug, nu = S5_UNIT_GROUPS, S5_UNITS
    eye = jnp.eye(ug, dtype=F32)

    diag = eye[:, None, :, None]

    def pack_b(t):
        t = t.reshape(2, nu, ug, S5_STATE, S5_GROUP_SIZE).transpose(0, 1, 2, 4, 3)
        return (t[:, :, :, :, None, :] * diag).reshape(2, nu, LANES, S5_UNIT_STATE)

    def pack_c(t):
        t = t.reshape(2, nu, ug, S5_GROUP_SIZE, S5_STATE).transpose(0, 1, 4, 2, 3)
        return (t[:, :, None] * diag).reshape(2, nu, S5_UNIT_STATE, LANES)

    def pack_pp(t):
        t = t.reshape(2, nu, ug, S5_GROUP_SIZE, S5_GROUP_SIZE).transpose(0, 1, 2, 4, 3)
        return (t[:, :, :, :, None, :] * diag).reshape(2, nu, LANES, LANES)

    ab_re = a_re[..., None] * bb_re - a_im[..., None] * bb_im
    ab_im = a_re[..., None] * bb_im + a_im[..., None] * bb_re
    b_plain = jnp.concatenate([pack_b(bb_re), pack_b(bb_im)], axis=-1)
    b_first = jnp.concatenate([pack_b(ab_re), pack_b(ab_im)], axis=-1)
    bu = jnp.stack([jnp.concatenate([b_first[0], b_plain[0]], axis=-2),
                    jnp.concatenate([b_plain[1], b_first[1]], axis=-2)]).astype(BF16)

    ca_re = c_re * a_re[:, :, None, :] - c_im * a_im[:, :, None, :]
    ca_im = c_re * a_im[:, :, None, :] + c_im * a_re[:, :, None, :]
    c_plain = jnp.concatenate([pack_c(c_re), -pack_c(c_im)], axis=-2)
    c_a = jnp.concatenate([pack_c(ca_re), -pack_c(ca_im)], axis=-2)
    cu = jnp.concatenate([c_plain, c_a], axis=-1).astype(BF16)

    cb_pq = (jnp.einsum('dgpn,dgnq->dgpq', c_re, bb_re) - jnp.einsum('dgpn,dgnq->dgpq', c_im, bb_im))
    cb_mat = pack_pp(cb_pq)
    zero = jnp.zeros_like(cb_mat[0])
    cb = jnp.stack([jnp.concatenate([cb_mat[0], zero], axis=-2),
                    jnp.concatenate([zero, cb_mat[1]], axis=-2)]).astype(BF16)

    a2_re = a_re * a_re - a_im * a_im
    a2_im = 2.0 * a_re * a_im
    a2 = jnp.stack([a2_re.reshape(2, -1), a2_im.reshape(2, -1)], axis=1)
    return bu, cu, cb, a2


POOL_HALO = SUBLANES
POOL_ROWS = 256


def _pool_mix(xs_ref, w_ref, s_ref, t0, seq, tm):
    n = POOL_ROWS + 2 * POOL_HALO
    out = []
    for c in range(tm // POOL_ROWS):
        t = t0 + POOL_ROWS * c + lax.broadcasted_iota(jnp.int32, (POOL_ROWS, 1), 0)
        groups = []
        for g, win in enumerate(POOL_WINDOWS):
            lanes = slice(POOL_GROUP_SIZE * g, POOL_GROUP_SIZE * (g + 1))
            x = xs_ref[POOL_ROWS * c:POOL_ROWS * c + n, lanes]
            s = x
            k = 1
            while k < win:
                s = s + pltpu.roll(s, k, axis=0)
                k *= 2
            left = win // 2
            right = win - 1 - left
            if right:
                s = pltpu.roll(s, n - right, axis=0)
            hi = jnp.minimum(t + right + 1, seq)
            lo = jnp.maximum(t - left, 0)
            mean = s[POOL_HALO:POOL_HALO + POOL_ROWS] / (hi - lo).astype(F32)
            pooled = mean - x[POOL_HALO:POOL_HALO + POOL_ROWS]
            mixed = jnp.dot(pooled.astype(BF16), w_ref[g], preferred_element_type=F32)
            groups.append((mixed * s_ref[:, lanes]).astype(BF16))
        out.append(jnp.concatenate(groups, axis=1))
    return jnp.concatenate(out, axis=0)


def _stat_lane(head):
    return head + HEAD_DIM * (1 - head % 2)


STAT_MAX_SHIFT = 8


def _out_pitch(dil):
    return dil + SUBLANES if dil % (2 * SUBLANES) == 0 else dil


def _attn_kernel(q_ref, k_ref, v_ref, bias_ref, o_ref, stat_ref, *scr, dil, m_rows, pairs, unroll):
    hc = pl.program_id(1)
    nqb = m_rows // QBLK
    pitch = _out_pitch(dil)
    stat = stat_ref if dil == 1 else scr[1]
    stat[...] = jnp.ones_like(stat)

    low = lax.broadcasted_iota(jnp.int32, (QBLK, LANES), 1) < HEAD_DIM

    def qblock(it, carry):
        r = it // nqb
        m0 = pl.multiple_of((it % nqb) * QBLK, QBLK)
        ks = pl.multiple_of(jnp.clip(m0 - QBLK // 2, 0, m_rows - KWIN), QBLK // 2)
        var = (m0 - ks) // (QBLK // 2)
        qrows = pl.ds(m0, QBLK)
        krows = pl.ds(ks, KWIN)
        srows = pl.ds(pl.multiple_of(r * m_rows + m0, QBLK), QBLK)
        for j in range(pairs):
            lanes = slice(LANES * j, LANES * (j + 1))
            qp = q_ref[r, qrows, lanes]
            zero = jnp.zeros_like(qp)
            q2 = jnp.concatenate([jnp.where(low, qp, zero), jnp.where(low, zero, qp)], axis=0)
            s2 = lax.dot_general(q2, k_ref[r, krows, lanes], (((1,), (1,)), ((), ())),
                                 preferred_element_type=F32)
            vp = v_ref[r, krows, lanes]
            halves = []
            for e in range(2):
                s = s2[QBLK * e:QBLK * (e + 1)] + bias_ref[var, 2 * (hc * pairs + j) + e]
                mx = jnp.max(s, axis=-1, keepdims=True)
                p = jnp.exp2(s - mx)
                halves.append(jnp.dot(p.astype(BF16), vp, preferred_element_type=F32))
                sl = _stat_lane(2 * j + e)
                stat[srows, sl:sl + 1] = jnp.sum(p, axis=-1, keepdims=True)
                stat[srows, sl + STAT_MAX_SHIFT:sl + STAT_MAX_SHIFT + 1] = mx
            o_pair = jnp.where(low, halves[0], halves[1])
            if dil == 1:
                o_ref[qrows, lanes] = o_pair.astype(BF16)
            else:
                scr[0][j, pl.ds(r + pitch * m0, QBLK, stride=pitch), :] = o_pair
        return carry

    lax.fori_loop(0, dil * nqb, qblock, 0, unroll=unroll)
    if dil > 1:
        for j in range(pairs):
            lanes = slice(LANES * j, LANES * (j + 1))
            if pitch == dil:
                o_ref[:, lanes] = scr[0][j].astype(BF16)
            else:
                for m in range(m_rows):
                    o_ref[dil * m:dil * (m + 1), lanes] = scr[0][j, pitch * m:pitch * m + dil, :].astype(BF16)
        for r in range(dil):
            rows = slice(r * m_rows, (r + 1) * m_rows)
            stat_ref[pl.ds(r, m_rows, stride=dil), :] = stat[rows, :]


def _attn_bias(dil):
    slopes = jnp.exp2(-8.0 * jnp.arange(1, ATTN_HEADS + 1, dtype=F32) / ATTN_HEADS)
    a = jnp.arange(QBLK)[:, None]
    c = jnp.arange(KWIN)[None, :]
    off = (jnp.arange(3) * (QBLK // 2))[:, None, None]
    rel = c[None] - off - a[None]
    dist = (dil * jnp.abs(rel)).astype(F32)
    bias = -LOG2E * slopes[None, :, None, None] * dist[:, None]
    return jnp.where((jnp.abs(rel) <= QBLK // 2)[:, None], bias, NEG_INF)


def _attn_pattern(q, k, v, bias, dil, bsz, seq, hw=ATTN_CHUNK, unroll=32):
    m_rows = seq // dil
    nhc = ATTN_WIDTH // hw
    pairs = hw // LANES
    blk = pl.BlockSpec((None, dil, m_rows, hw), lambda b, h: (b, 0, 0, h))
    stat = pl.BlockSpec((seq, LANES), lambda b, h: (b, h))
    regroup = [pltpu.VMEM((pairs, m_rows * _out_pitch(dil), LANES), F32), pltpu.VMEM((seq, LANES), F32)]
    return pl.pallas_call(
        functools.partial(_attn_kernel, dil=dil, m_rows=m_rows, pairs=pairs, unroll=unroll),
        grid=(bsz, nhc),
        in_specs=[blk, blk, blk,
                  pl.BlockSpec(bias.shape, lambda b, h: (0, 0, 0, 0), pipeline_mode=pl.Buffered(1))],
        out_specs=[pl.BlockSpec((seq, hw), lambda b, h: (b, h)), stat],
        out_shape=[jax.ShapeDtypeStruct((bsz * seq, ATTN_WIDTH), BF16),
                   jax.ShapeDtypeStruct((bsz * seq, nhc * LANES), F32)],
        scratch_shapes=regroup if dil > 1 else [],
        compiler_params=_cparams("parallel", "parallel"),
        name=f"attn_d{dil}",
    )(q, k, v, bias)


def _merge_kernel(yf_ref, yb_ref, us5_ref, dsk_ref, wglu_ref,
                  up_ref, uprev_ref, unext_ref, pw_ref, ps_ref,
                  o1_ref, o2_ref, o3_ref, s1_ref, s2_ref, s3_ref, exp_ref,
                  gate_ref, h_ref, wbs_ref, wbp_ref, wba_ref, wout_ref, out_ref, xs_ref, *, seq):
    dot = functools.partial(jnp.dot, preferred_element_type=F32)
    tm = h_ref.shape[0]
    i = pl.program_id(1)
    y = (yf_ref[...].astype(F32) + yb_ref[...].astype(F32)
         + dsk_ref[...] * us5_ref[...].astype(F32))
    y = jax.nn.gelu(y)
    y = y * _sigmoid(dot(y.astype(BF16), wglu_ref[...]))
    br_s5 = dot(y.astype(BF16), wbs_ref[...])

    xs_ref[0:POOL_HALO, :] = jnp.where(i > 0, uprev_ref[...], 0.0)
    xs_ref[POOL_HALO:POOL_HALO + tm, :] = up_ref[...]
    xs_ref[POOL_HALO + tm:, :] = jnp.where(i < pl.num_programs(1) - 1, unext_ref[...], 0.0)
    br_pool = dot(_pool_mix(xs_ref, pw_ref, ps_ref, i * tm, seq, tm), wbp_ref[...])

    lane = lax.broadcasted_iota(jnp.int32, (h_ref.shape[0], LANES), 1)
    is_stat = functools.reduce(jnp.logical_or,
                               [lane == _stat_lane(hd) for hd in range(ATTN_CHUNK // HEAD_DIM)])
    att = []
    for c in range(ATTN_NCHUNK):
        st, ch = slice(LANES * c, LANES * (c + 1)), slice(ATTN_CHUNK * c, ATTN_CHUNK * (c + 1))
        d1, d2, d3 = s1_ref[:, st], s2_ref[:, st], s3_ref[:, st]
        m1, m2, m3 = (pltpu.roll(s, LANES - STAT_MAX_SHIFT, axis=1) for s in (d1, d2, d3))
        mx = jnp.maximum(jnp.maximum(m1, m2), m3)
        w1, w2, w3 = jnp.exp2(m1 - mx), jnp.exp2(m2 - mx), jnp.exp2(m3 - mx)
        inv = 1.0 / (w1 * d1 + w2 * d2 + w3 * d3)
        expand = lambda w: dot(jnp.where(is_stat, w * inv, 0.0).astype(BF16), exp_ref[...]).astype(BF16)
        att.append(expand(w1) * o1_ref[:, ch] + expand(w2) * o2_ref[:, ch] + expand(w3) * o3_ref[:, ch])
    br_attn = dot(jnp.concatenate(att, axis=1), wba_ref[...])

    merged = (gate_ref[:, 0:D_MODEL] * br_s5.astype(BF16)
              + gate_ref[:, D_MODEL:2 * D_MODEL] * br_pool.astype(BF16)
              + gate_ref[:, 2 * D_MODEL:3 * D_MODEL] * br_attn.astype(BF16))
    out_ref[...] = h_ref[...] + dot(merged, wout_ref[...])


def _merge(y_dirs, us5_tm, dsk, wglu, u_pool, pool_w, pool_scale, outs, stats, gates, h2d,
           wbs, wbp, wba, wout, layer, bsz, seq, tm=512):
    T = bsz * seq
    tps = seq // tm
    hpt = tm // POOL_HALO
    row = lambda w: pl.BlockSpec((tm, w), lambda b, i: (b * tps + i, 0))
    full = lambda a: _layer_spec(a, layer)
    ydir = lambda d: pl.BlockSpec((None, tm, S5_WIDTH), lambda b, i: (d, i, b))
    halo_prev = pl.BlockSpec((POOL_HALO, POOL_WIDTH),
                             lambda b, i: (jnp.maximum((b * tps + i) * hpt - 1, 0), 0))
    halo_next = pl.BlockSpec((POOL_HALO, POOL_WIDTH),
                             lambda b, i: (jnp.minimum((b * tps + i + 1) * hpt, T // POOL_HALO - 1), 0))
    local_head = jnp.arange(ATTN_CHUNK)[None, :] // HEAD_DIM
    expand = (jnp.arange(LANES)[:, None] == _stat_lane(local_head)).astype(BF16)
    return pl.pallas_call(
        functools.partial(_merge_kernel, seq=seq),
        grid=(bsz, tps),
        in_specs=[ydir(0), ydir(1),
                  pl.BlockSpec((tm, S5_WIDTH), lambda b, i: (i, b)),
                  full(dsk), full(wglu),
                  row(POOL_WIDTH), halo_prev, halo_next, full(pool_w), full(pool_scale),
                  row(ATTN_WIDTH), row(ATTN_WIDTH), row(ATTN_WIDTH),
                  *([row(ATTN_NCHUNK * LANES)] * 3),
                  pl.BlockSpec(expand.shape, lambda b, i: (0, 0)),
                  row(3 * D_MODEL), row(D_MODEL),
                  full(wbs), full(wbp), full(wba), full(wout)],
        out_specs=row(D_MODEL),
        out_shape=jax.ShapeDtypeStruct((T, D_MODEL), F32),
        scratch_shapes=[pltpu.VMEM((tm + 2 * POOL_HALO, POOL_WIDTH), F32)],
        compiler_params=_cparams("parallel", "parallel"),
        name="merge",
    )(y_dirs, y_dirs, us5_tm, dsk, wglu, u_pool, u_pool, u_pool, pool_w, pool_scale,
      *outs, *stats, expand, gates, h2d, wbs, wbp, wba, wout)


def _mlp_kernel(h_ref, g_ref, wup_ref, wdn_ref, gf_ref, o_ref, *, final, chunk):
    h = h_ref[...]
    hn = _rms(h, g_ref[...]).astype(BF16)
    acc = h
    for c in range(D_FF // chunk):
        cols = slice(chunk * c, chunk * (c + 1))
        up = jnp.dot(hn, wup_ref[:, cols], preferred_element_type=F32)
        act = jnp.square(jnp.maximum(up, 0.0)).astype(BF16)
        acc = acc + jnp.dot(act, wdn_ref[cols, :], preferred_element_type=F32)
    if final:
        acc = _rms(acc, gf_ref[...])
    o_ref[...] = acc


def _mlp(h2d, g, wup, wdn, gf, layer, final, tm=1024, chunk=1024):
    T = h2d.shape[0]
    const = lambda a: _layer_spec(a, layer, pipeline_mode=pl.Buffered(1))
    return pl.pallas_call(
        functools.partial(_mlp_kernel, final=final, chunk=chunk),
        grid=(T // tm,),
        in_specs=[pl.BlockSpec((tm, D_MODEL), lambda m: (m, 0)),
                  const(g), const(wup), const(wdn),
                  pl.BlockSpec(gf.shape, lambda m: (0, 0), pipeline_mode=pl.Buffered(1))],
        out_specs=pl.BlockSpec((tm, D_MODEL), lambda m: (m, 0)),
        out_shape=jax.ShapeDtypeStruct((T, D_MODEL), F32),
        compiler_params=_cparams("parallel"),
        name="mlp",
    )(h2d, g, wup, wdn, gf)


def kernel(x, norm_mix, w_in, s5_lam_re, s5_lam_im, s5_log_dt, s5_b_re, s5_b_im, s5_c_re, s5_c_im, s5_d, s5_w_glu, pool_w, pool_scale, w_branch_s5, w_branch_pool, w_branch_attn, w_out, norm_mlp, w_up, w_down, norm_final):
    bsz, seq, _ = x.shape
    depth = w_in.shape[0]
    bf = lambda t: t.astype(BF16)
    rows = lambda t: t.astype(F32).reshape(depth, 1, -1)
    h = x.astype(F32).reshape(bsz * seq, D_MODEL)
    s5_w = jax.vmap(_s5_params)(s5_lam_re, s5_lam_im, s5_log_dt, s5_b_re, s5_b_im, s5_c_re, s5_c_im)
    biases = [_attn_bias(dil) for dil in DILATIONS]
    norm_mix, norm_mlp, s5_d, pool_scale = map(rows, (norm_mix, norm_mlp, s5_d, pool_scale))
    (w_in, s5_w_glu, pool_w, w_branch_s5, w_branch_pool, w_branch_attn, w_out, w_up, w_down) = map(
        bf, (w_in, s5_w_glu, pool_w, w_branch_s5, w_branch_pool, w_branch_attn, w_out, w_up, w_down))
    norm_final = norm_final.astype(F32).reshape(1, -1)
    for l in range(depth):
        us5_tm, u_pool, q, k, v, gates = _inproj(h, norm_mix, w_in, l, bsz, seq)
        y_dirs = _s5_scan(us5_tm, *s5_w, l, bsz, seq)
        outs, stats = zip(*[_attn_pattern(q[i], k[i], v[i], biases[i], dil, bsz, seq)
                            for i, dil in enumerate(DILATIONS)])
        h = _merge(y_dirs, us5_tm, s5_d, s5_w_glu, u_pool, pool_w, pool_scale, outs, stats, gates, h,
                   w_branch_s5, w_branch_pool, w_branch_attn, w_out, l, bsz, seq)
        h = _mlp(h, norm_mlp, w_up, w_down, norm_final, l, l == depth - 1)
    return h.reshape(bsz, seq, D_MODEL).astype(x.dtype)
```

```python
import functools
import math

import jax
import jax.numpy as jnp
import numpy as np
from jax import lax
from jax.experimental import pallas as pl
from jax.experimental.pallas import tpu as pltpu

F32 = jnp.float32
BF16 = jnp.bfloat16

D_MODEL = 1024
S5_WIDTH = 512
S5_GROUP_SIZE = 16
S5_STATE = 64
POOL_WIDTH = 512
POOL_WINDOWS = (2, 4, 8, 16)
POOL_GROUP_SIZE = 128
ATTN_HEADS = 16
HEAD_DIM = 64
ATTN_WIDTH = 1024
ATTN_PATTERNS = ((128, 1), (512, 4), (2048, 16))
D_FF = 4096
NORM_EPS = 1e-6
NEG_INF = -1e30

LANES = 128
SUBLANES = 8
S5_UNIT_GROUPS = LANES // S5_GROUP_SIZE
S5_UNITS = S5_WIDTH // LANES
S5_UNIT_STATE = S5_UNIT_GROUPS * S5_STATE
QBLK = 128
KWIN = 2 * QBLK
ATTN_CHUNK = 256
ATTN_NCHUNK = ATTN_WIDTH // ATTN_CHUNK
VMEM_LIMIT = 56 * 1024 * 1024


def _cparams(*sem):
    return pltpu.CompilerParams(dimension_semantics=sem, vmem_limit_bytes=VMEM_LIMIT)


def _rms(x, g):
    ms = jnp.mean(x * x, axis=-1, keepdims=True)
    return x * lax.rsqrt(ms + NORM_EPS) * g


def _layer_spec(a, layer, **kw):
    rest = (0,) * (a.ndim - 1)
    return pl.BlockSpec((None,) + a.shape[1:], lambda *_: (layer,) + rest, **kw)


def _sigmoid(x):
    return 0.5 + 0.5 * jnp.tanh(0.5 * x)


DILATIONS = tuple(dil for _, dil in ATTN_PATTERNS)
LOG2E = math.log2(math.e)
Q_SCALE = HEAD_DIM ** -0.5 * LOG2E


def _inproj_kernel(x_ref, g_ref, w_ref, us5_ref, upool_ref, *rest, tm):
    nd = len(DILATIONS)
    qkv_refs = [rest[nd * i:nd * (i + 1)] for i in range(3)]
    gate_ref = rest[3 * nd]
    lay_refs = dict(zip(DILATIONS[:-1], rest[3 * nd + 1:]))
    xn = _rms(x_ref[...], g_ref[...]).astype(BF16)
    proj = lambda c: jnp.dot(xn, w_ref[:, D_MODEL * c:D_MODEL * (c + 1)], preferred_element_type=F32)

    for c in range(3):
        gate_ref[:, D_MODEL * c:D_MODEL * (c + 1)] = _sigmoid(proj(4 + c)).astype(BF16)

    for i, refs in enumerate(qkv_refs):
        acc = proj(1 + i)
        if i == 0:
            acc = acc * Q_SCALE
        refs[0][0] = acc.astype(BF16)
        for j in range(ATTN_WIDTH // LANES):
            lay_refs[1][j] = acc[:, LANES * j:LANES * (j + 1)]
        for base, dil, ref in zip(DILATIONS[:-1], DILATIONS[1:], refs[1:]):
            ratio, rows = dil // base, tm // dil
            for c in range(base):
                for a in range(ratio):
                    r = c + base * a
                    for j in range(ATTN_WIDTH // LANES):
                        piece = lay_refs[base][j, pl.ds(c * (tm // base) + a, rows, stride=ratio), :]
                        ref[r, :, LANES * j:LANES * (j + 1)] = piece.astype(BF16)
                        if dil in lay_refs:
                            lay_refs[dil][j, r * rows:(r + 1) * rows, :] = piece

    acc = proj(0)
    us5_ref[...] = acc[:, :S5_WIDTH].astype(BF16)
    upool_ref[...] = acc[:, S5_WIDTH:]


def _inproj(h2d, g, w_bf16, layer, bsz, seq, tm=512):
    T = bsz * seq
    tps = seq // tm
    row = lambda m: (m, 0)
    const = lambda a: _layer_spec(a, layer, pipeline_mode=pl.Buffered(1))
    lay_specs = [pl.BlockSpec((None, dil, tm // dil, ATTN_WIDTH), lambda m: (m // tps, 0, m % tps, 0))
                 for dil in DILATIONS]
    lay_shapes = [jax.ShapeDtypeStruct((bsz, dil, seq // dil, ATTN_WIDTH), BF16) for dil in DILATIONS]
    nd = len(DILATIONS)
    outs = pl.pallas_call(
        functools.partial(_inproj_kernel, tm=tm),
        grid=(T // tm,),
        in_specs=[pl.BlockSpec((tm, D_MODEL), row), const(g), const(w_bf16)],
        out_specs=[
            pl.BlockSpec((tm, S5_WIDTH), lambda m: (m % tps, m // tps)),
            pl.BlockSpec((tm, POOL_WIDTH), row),
            *(lay_specs * 3),
            pl.BlockSpec((tm, 3 * D_MODEL), row),
        ],
        out_shape=[
            jax.ShapeDtypeStruct((seq, bsz * S5_WIDTH), BF16),
            jax.ShapeDtypeStruct((T, POOL_WIDTH), F32),
            *(lay_shapes * 3),
            jax.ShapeDtypeStruct((T, 3 * D_MODEL), BF16),
        ],
        scratch_shapes=[pltpu.VMEM((ATTN_WIDTH // LANES, tm, LANES), F32)] * (nd - 1),
        compiler_params=_cparams("parallel"),
        name="inproj",
    )(h2d, g, w_bf16)
    q, k, v = (outs[2 + nd * i:2 + nd * (i + 1)] for i in range(3))
    return outs[0], outs[1], q, k, v, outs[2 + 3 * nd]


def _s5_kernel(u_ref, bu_ref, cu_ref, cb_ref, a_ref, y_ref, st_ref, tmp_ref, xe_ref, xo_ref,
               p_ref, ypar_ref, ynat_ref, *sig_refs, tt, bsz):
    d = pl.program_id(0)
    i = pl.program_id(1)
    hs = S5_UNIT_STATE
    npair = tt // 2
    prows = npair * bsz
    nu = S5_UNITS

    @pl.when(i == 0)
    def _():
        st_ref[...] = jnp.zeros_like(st_ref)

    for b in range(bsz):
        for j in range(nu):
            lanes = slice(S5_WIDTH * b + LANES * j, S5_WIDTH * b + LANES * (j + 1))
            tmp_ref[b * nu + j] = u_ref[:, lanes].astype(F32)
            xe_ref[j, pl.ds(b, npair, stride=bsz), :] = tmp_ref[b * nu + j, pl.ds(0, npair, stride=2), :]
            xo_ref[j, pl.ds(b, npair, stride=bsz), :] = tmp_ref[b * nu + j, pl.ds(1, npair, stride=2), :]

    lead = pl.multiple_of((1 - d) * bsz, bsz)
    carry_rows = pl.ds(pl.multiple_of(d * prows, bsz), bsz)
    x2 = []
    for j in range(nu):
        x2.append(jnp.concatenate([xe_ref[j], xo_ref[j]], axis=1).astype(BF16))
        sig_refs[j][pl.ds(lead, prows), :] = jnp.dot(x2[j], bu_ref[j], preferred_element_type=F32)
        sig_refs[j][carry_rows, :] = st_ref[:, 2 * hs * j:2 * hs * (j + 1)]

    for j0 in range(0, nu, 2):
        units = (j0, j0 + 1)
        coef = [(jnp.broadcast_to(a_ref[0:1, hs * j:hs * (j + 1)], (bsz, hs)),
                 jnp.broadcast_to(a_ref[1:2, hs * j:hs * (j + 1)], (bsz, hs))) for j in units]
        init = tuple((st_ref[:, 2 * hs * j:2 * hs * j + hs], st_ref[:, 2 * hs * j + hs:2 * hs * (j + 1)])
                     for j in units)

        def step(s, carry, units=units, coef=coef):
            m = jnp.where(d == 0, s, npair - 1 - s)
            rows = pl.ds(pl.multiple_of(m * bsz + lead, bsz), bsz)
            out = []
            for j, (ar, ai), (sr, si) in zip(units, coef, carry):
                sig = sig_refs[j]
                nr = ar * sr - ai * si + sig[rows, :hs]
                ni = ar * si + ai * sr + sig[rows, hs:]
                sig[rows, :hs] = nr
                sig[rows, hs:] = ni
                out.append((nr, ni))
            return tuple(out)

        final = lax.fori_loop(0, npair, step, init, unroll=4)
        for j, (sr, si) in zip(units, final):
            st_ref[:, 2 * hs * j:2 * hs * j + hs] = sr
            st_ref[:, 2 * hs * j + hs:2 * hs * (j + 1)] = si

    for j in range(nu):
        p_ref[j] = jnp.dot(sig_refs[j][...].astype(BF16), cu_ref[j], preferred_element_type=F32)
        ypar_ref[0, j] = p_ref[j, pl.ds(lead, prows), :LANES]
        ypar_ref[1, j] = (p_ref[j, pl.ds(bsz - lead, prows), LANES:]
                          + jnp.dot(x2[j], cb_ref[j], preferred_element_type=F32))

    for b in range(bsz):
        for j in range(nu):
            lanes = slice(S5_WIDTH * b + LANES * j, S5_WIDTH * b + LANES * (j + 1))
            k = b * nu + j
            ynat_ref[k, pl.ds(1 - d, npair, stride=2), :] = ypar_ref[0, j, pl.ds(b, npair, stride=bsz), :]
            ynat_ref[k, pl.ds(d, npair, stride=2), :] = ypar_ref[1, j, pl.ds(b, npair, stride=bsz), :]
            y_ref[:, lanes] = ynat_ref[k].astype(BF16)


def _s5_scan(u_tm, bu, cu, cb, a2, layer, bsz, seq, tt=128):
    assert bsz == SUBLANES and tt % 2 == 0
    n_t = seq // tt
    tile = lambda d, i: i + d * (n_t - 1 - 2 * i)
    uw = 2 * S5_UNIT_STATE
    prows = tt // 2 * bsz
    per_dir = lambda a: pl.BlockSpec((None, None) + a.shape[2:],
                                     lambda d, i: (layer, d) + (0,) * (a.ndim - 2))
    return pl.pallas_call(
        functools.partial(_s5_kernel, tt=tt, bsz=bsz),
        grid=(2, n_t),
        in_specs=[pl.BlockSpec((tt, bsz * S5_WIDTH), lambda d, i: (tile(d, i), 0)),
                  per_dir(bu), per_dir(cu), per_dir(cb), per_dir(a2)],
        out_specs=pl.BlockSpec((None, tt, bsz * S5_WIDTH), lambda d, i: (d, tile(d, i), 0)),
        out_shape=jax.ShapeDtypeStruct((2, seq, bsz * S5_WIDTH), BF16),
        scratch_shapes=[pltpu.VMEM((bsz, S5_UNITS * uw), F32),
                        pltpu.VMEM((bsz * S5_UNITS, tt, LANES), F32),
                        pltpu.VMEM((S5_UNITS, prows, LANES), F32),
                        pltpu.VMEM((S5_UNITS, prows, LANES), F32),
                        pltpu.VMEM((S5_UNITS, prows + bsz, 2 * LANES), F32),
                        pltpu.VMEM((2, S5_UNITS, prows, LANES), F32),
                        pltpu.VMEM((bsz * S5_UNITS, tt, LANES), F32)]
        + [pltpu.VMEM((prows + bsz, uw), F32)] * S5_UNITS,
        compiler_params=_cparams("arbitrary", "arbitrary"),
        name="s5_scan",
    )(u_tm, bu, cu, cb, a2)


def _s5_params(lam_re, lam_im, log_dt, b_re, b_im, c_re, c_im):
    f = lambda t: t.astype(F32)
    lam_re, lam_im, b_re, b_im, c_re, c_im = map(f, (lam_re, lam_im, b_re, b_im, c_re, c_im))
    dt = jnp.exp(f(log_dt))[:, :, None]
    mag = jnp.exp(lam_re * dt)
    a_re = mag * jnp.cos(lam_im * dt)
    a_im = mag * jnp.sin(lam_im * dt)
    den = lam_re * lam_re + lam_im * lam_im
    f_re = ((a_re - 1.0) * lam_re + a_im * lam_im) / den
    f_im = (a_im * lam_re - (a_re - 1.0) * lam_im) / den
    bb_re = f_re[..., None] * b_re - f_im[..., None] * b_im
    bb_im = f_re[..., None] * b_im + f_im[..., None] * b_re
    ug, nu = S5_UNIT_GROUPS, S5_UNITS

    def block_diag(t, rows_per_group):
        r, cols = t.shape[2], t.shape[3]
        same = (np.arange(ug * r)[:, None] // rows_per_group
                == np.arange(cols)[None, :] // (cols // ug)).astype(np.float32)
        return jnp.broadcast_to(t[:, :, None], (2, nu, ug, r, cols)).reshape(2, nu, ug * r, cols) * same

    def pack_b(t):
        t = t.reshape(2, nu, ug, S5_STATE, S5_GROUP_SIZE).transpose(0, 1, 4, 2, 3)
        return block_diag(t.reshape(2, nu, S5_GROUP_SIZE, S5_UNIT_STATE), S5_GROUP_SIZE)

    def pack_c(t):
        t = t.reshape(2, nu, ug, S5_GROUP_SIZE, S5_STATE).transpose(0, 1, 4, 2, 3)
        return block_diag(t.reshape(2, nu, S5_STATE, LANES), S5_STATE)

    def pack_pp(t):
        t = t.reshape(2, nu, ug, S5_GROUP_SIZE, S5_GROUP_SIZE).transpose(0, 1, 4, 2, 3)
        return block_diag(t.reshape(2, nu, S5_GROUP_SIZE, LANES), S5_GROUP_SIZE)

    ab_re = a_re[..., None] * bb_re - a_im[..., None] * bb_im
    ab_im = a_re[..., None] * bb_im + a_im[..., None] * bb_re
    b_plain = jnp.concatenate([pack_b(bb_re), pack_b(bb_im)], axis=-1)
    b_first = jnp.concatenate([pack_b(ab_re), pack_b(ab_im)], axis=-1)
    bu = jnp.stack([jnp.concatenate([b_first[0], b_plain[0]], axis=-2),
                    jnp.concatenate([b_plain[1], b_first[1]], axis=-2)]).astype(BF16)

    ca_re = c_re * a_re[:, :, None, :] - c_im * a_im[:, :, None, :]
    ca_im = c_re * a_im[:, :, None, :] + c_im * a_re[:, :, None, :]
    c_plain = jnp.concatenate([pack_c(c_re), -pack_c(c_im)], axis=-2)
    c_a = jnp.concatenate([pack_c(ca_re), -pack_c(ca_im)], axis=-2)
    cu = jnp.concatenate([c_plain, c_a], axis=-1).astype(BF16)

    cb_pq = (jnp.einsum('dgpn,dgnq->dgpq', c_re, bb_re) - jnp.einsum('dgpn,dgnq->dgpq', c_im, bb_im))
    cb_mat = pack_pp(cb_pq)
    zero = jnp.zeros_like(cb_mat[0])
    cb = jnp.stack([jnp.concatenate([cb_mat[0], zero], axis=-2),
                    jnp.concatenate([zero, cb_mat[1]], axis=-2)]).astype(BF16)

    a2_re = a_re * a_re - a_im * a_im
    a2_im = 2.0 * a_re * a_im
    a2 = jnp.stack([a2_re.reshape(2, -1), a2_im.reshape(2, -1)], axis=1)
    return bu, cu, cb, a2


POOL_HALO = SUBLANES
POOL_ROWS = 256


def _pool_mix(xs_ref, w_ref, s_ref, t0, seq, tm):
    n = POOL_ROWS + 2 * POOL_HALO
    out = []
    for c in range(tm // POOL_ROWS):
        t = t0 + POOL_ROWS * c + lax.broadcasted_iota(jnp.int32, (POOL_ROWS, 1), 0)
        groups = []
        for g, win in enumerate(POOL_WINDOWS):
            lanes = slice(POOL_GROUP_SIZE * g, POOL_GROUP_SIZE * (g + 1))
            x = xs_ref[POOL_ROWS * c:POOL_ROWS * c + n, lanes]
            s = x
            k = 1
            while k < win:
                s = s + pltpu.roll(s, k, axis=0)
                k *= 2
            left = win // 2
            right = win - 1 - left
            if right:
                s = pltpu.roll(s, n - right, axis=0)
            hi = jnp.minimum(t + right + 1, seq)
            lo = jnp.maximum(t - left, 0)
            mean = s[POOL_HALO:POOL_HALO + POOL_ROWS] / (hi - lo).astype(F32)
            pooled = mean - x[POOL_HALO:POOL_HALO + POOL_ROWS]
            mixed = jnp.dot(pooled.astype(BF16), w_ref[g], preferred_element_type=F32)
            groups.append((mixed * s_ref[:, lanes]).astype(BF16))
        out.append(jnp.concatenate(groups, axis=1))
    return jnp.concatenate(out, axis=0)


def _stat_lane(head):
    return head + HEAD_DIM * (1 - head % 2)


STAT_MAX_SHIFT = 8


def _out_pitch(dil):
    return dil + SUBLANES if dil % (2 * SUBLANES) == 0 else dil


def _attn_kernel(q_ref, k_ref, v_ref, bias_ref, o_ref, stat_ref, *scr, dil, m_rows, pairs, unroll):
    hc = pl.program_id(1)
    nqb = m_rows // QBLK
    pitch = _out_pitch(dil)
    stat = stat_ref if dil == 1 else scr[1]
    stat[...] = jnp.ones_like(stat)

    low = lax.broadcasted_iota(jnp.int32, (QBLK, LANES), 1) < HEAD_DIM

    def qblock(it, carry):
        r = it // nqb
        m0 = pl.multiple_of((it % nqb) * QBLK, QBLK)
        ks = pl.multiple_of(jnp.clip(m0 - QBLK // 2, 0, m_rows - KWIN), QBLK // 2)
        var = (m0 - ks) // (QBLK // 2)
        qrows = pl.ds(m0, QBLK)
        krows = pl.ds(ks, KWIN)
        srows = pl.ds(pl.multiple_of(r * m_rows + m0, QBLK), QBLK)
        for j in range(pairs):
            lanes = slice(LANES * j, LANES * (j + 1))
            qp = q_ref[r, qrows, lanes]
            zero = jnp.zeros_like(qp)
            q2 = jnp.concatenate([jnp.where(low, qp, zero), jnp.where(low, zero, qp)], axis=0)
            s2 = lax.dot_general(q2, k_ref[r, krows, lanes], (((1,), (1,)), ((), ())),
                                 preferred_element_type=F32)
            vp = v_ref[r, krows, lanes]
            halves = []
            for e in range(2):
                s = s2[QBLK * e:QBLK * (e + 1)] + bias_ref[var, 2 * (hc * pairs + j) + e]
                mx = jnp.max(s, axis=-1, keepdims=True)
                p = jnp.exp2(s - mx)
                halves.append(jnp.dot(p.astype(BF16), vp, preferred_element_type=F32))
                sl = _stat_lane(2 * j + e)
                stat[srows, sl:sl + 1] = jnp.sum(p, axis=-1, keepdims=True)
                stat[srows, sl + STAT_MAX_SHIFT:sl + STAT_MAX_SHIFT + 1] = mx
            o_pair = jnp.where(low, halves[0], halves[1])
            if dil == 1:
                o_ref[qrows, lanes] = o_pair.astype(BF16)
            else:
                scr[0][j, pl.ds(r + pitch * m0, QBLK, stride=pitch), :] = o_pair
        return carry

    lax.fori_loop(0, dil * nqb, qblock, 0, unroll=unroll)
    if dil > 1:
        for j in range(pairs):
            lanes = slice(LANES * j, LANES * (j + 1))
            if pitch == dil:
                o_ref[:, lanes] = scr[0][j].astype(BF16)
            else:
                for m in range(m_rows):
                    o_ref[dil * m:dil * (m + 1), lanes] = scr[0][j, pitch * m:pitch * m + dil, :].astype(BF16)
        for r in range(dil):
            rows = slice(r * m_rows, (r + 1) * m_rows)
            stat_ref[pl.ds(r, m_rows, stride=dil), :] = stat[rows, :]


def _attn_bias(dil):
    slopes = np.exp2(-8.0 * np.arange(1, ATTN_HEADS + 1, dtype=np.float32) / ATTN_HEADS)
    a = np.arange(QBLK)[:, None]
    c = np.arange(KWIN)[None, :]
    off = (np.arange(3) * (QBLK // 2))[:, None, None]
    rel = c[None] - off - a[None]
    dist = (dil * np.abs(rel)).astype(np.float32)
    bias = np.float32(-LOG2E) * slopes[None, :, None, None] * dist[:, None]
    bias = np.where((np.abs(rel) <= QBLK // 2)[:, None], bias, np.float32(NEG_INF))
    return jnp.asarray(bias.astype(np.float32))


def _attn_pattern(q, k, v, bias, dil, bsz, seq, hw=ATTN_CHUNK, unroll=32):
    m_rows = seq // dil
    nhc = ATTN_WIDTH // hw
    pairs = hw // LANES
    blk = pl.BlockSpec((None, dil, m_rows, hw), lambda b, h: (b, 0, 0, h))
    stat = pl.BlockSpec((seq, LANES), lambda b, h: (b, h))
    regroup = [pltpu.VMEM((pairs, m_rows * _out_pitch(dil), LANES), F32), pltpu.VMEM((seq, LANES), F32)]
    return pl.pallas_call(
        functools.partial(_attn_kernel, dil=dil, m_rows=m_rows, pairs=pairs, unroll=unroll),
        grid=(bsz, nhc),
        in_specs=[blk, blk, blk,
                  pl.BlockSpec(bias.shape, lambda b, h: (0, 0, 0, 0), pipeline_mode=pl.Buffered(1))],
        out_specs=[pl.BlockSpec((seq, hw), lambda b, h: (b, h)), stat],
        out_shape=[jax.ShapeDtypeStruct((bsz * seq, ATTN_WIDTH), BF16),
                   jax.ShapeDtypeStruct((bsz * seq, nhc * LANES), F32)],
        scratch_shapes=regroup if dil > 1 else [],
        compiler_params=_cparams("parallel", "parallel"),
        name=f"attn_d{dil}",
    )(q, k, v, bias)


def _merge_kernel(yf_ref, yb_ref, us5_ref, dsk_ref, wglu_ref,
                  up_ref, uprev_ref, unext_ref, pw_ref, ps_ref,
                  o1_ref, o2_ref, o3_ref, s1_ref, s2_ref, s3_ref, exp_ref,
                  gate_ref, h_ref, wbs_ref, wbp_ref, wba_ref, wout_ref, out_ref, xs_ref, *, seq):
    dot = functools.partial(jnp.dot, preferred_element_type=F32)
    tm = h_ref.shape[0]
    i = pl.program_id(1)
    y = (yf_ref[...].astype(F32) + yb_ref[...].astype(F32)
         + dsk_ref[...] * us5_ref[...].astype(F32))
    y = jax.nn.gelu(y)
    y = y * _sigmoid(dot(y.astype(BF16), wglu_ref[...]))
    br_s5 = dot(y.astype(BF16), wbs_ref[...])

    xs_ref[0:POOL_HALO, :] = jnp.where(i > 0, uprev_ref[...], 0.0)
    xs_ref[POOL_HALO:POOL_HALO + tm, :] = up_ref[...]
    xs_ref[POOL_HALO + tm:, :] = jnp.where(i < pl.num_programs(1) - 1, unext_ref[...], 0.0)
    br_pool = dot(_pool_mix(xs_ref, pw_ref, ps_ref, i * tm, seq, tm), wbp_ref[...])

    lane = lax.broadcasted_iota(jnp.int32, (h_ref.shape[0], LANES), 1)
    is_stat = functools.reduce(jnp.logical_or,
                               [lane == _stat_lane(hd) for hd in range(ATTN_CHUNK // HEAD_DIM)])
    att = []
    for c in range(ATTN_NCHUNK):
        st, ch = slice(LANES * c, LANES * (c + 1)), slice(ATTN_CHUNK * c, ATTN_CHUNK * (c + 1))
        d1, d2, d3 = s1_ref[:, st], s2_ref[:, st], s3_ref[:, st]
        m1, m2, m3 = (pltpu.roll(s, LANES - STAT_MAX_SHIFT, axis=1) for s in (d1, d2, d3))
        mx = jnp.maximum(jnp.maximum(m1, m2), m3)
        w1, w2, w3 = jnp.exp2(m1 - mx), jnp.exp2(m2 - mx), jnp.exp2(m3 - mx)
        inv = 1.0 / (w1 * d1 + w2 * d2 + w3 * d3)
        expand = lambda w: dot(jnp.where(is_stat, w * inv, 0.0).astype(BF16), exp_ref[...]).astype(BF16)
        att.append(expand(w1) * o1_ref[:, ch] + expand(w2) * o2_ref[:, ch] + expand(w3) * o3_ref[:, ch])
    br_attn = dot(jnp.concatenate(att, axis=1), wba_ref[...])

    merged = (gate_ref[:, 0:D_MODEL] * br_s5.astype(BF16)
              + gate_ref[:, D_MODEL:2 * D_MODEL] * br_pool.astype(BF16)
              + gate_ref[:, 2 * D_MODEL:3 * D_MODEL] * br_attn.astype(BF16))
    out_ref[...] = h_ref[...] + dot(merged, wout_ref[...])


def _merge(y_dirs, us5_tm, dsk, wglu, u_pool, pool_w, pool_scale, outs, stats, gates, h2d,
           wbs, wbp, wba, wout, layer, bsz, seq, tm=512):
    T = bsz * seq
    tps = seq // tm
    hpt = tm // POOL_HALO
    row = lambda w: pl.BlockSpec((tm, w), lambda b, i: (b * tps + i, 0))
    full = lambda a: _layer_spec(a, layer)
    ydir = lambda d: pl.BlockSpec((None, tm, S5_WIDTH), lambda b, i: (d, i, b))
    halo_prev = pl.BlockSpec((POOL_HALO, POOL_WIDTH),
                             lambda b, i: (jnp.maximum((b * tps + i) * hpt - 1, 0), 0))
    halo_next = pl.BlockSpec((POOL_HALO, POOL_WIDTH),
                             lambda b, i: (jnp.minimum((b * tps + i + 1) * hpt, T // POOL_HALO - 1), 0))
    local_head = np.arange(ATTN_CHUNK)[None, :] // HEAD_DIM
    expand = jnp.asarray(np.arange(LANES)[:, None] == _stat_lane(local_head), BF16)
    return pl.pallas_call(
        functools.partial(_merge_kernel, seq=seq),
        grid=(bsz, tps),
        in_specs=[ydir(0), ydir(1),
                  pl.BlockSpec((tm, S5_WIDTH), lambda b, i: (i, b)),
                  full(dsk), full(wglu),
                  row(POOL_WIDTH), halo_prev, halo_next, full(pool_w), full(pool_scale),
                  row(ATTN_WIDTH), row(ATTN_WIDTH), row(ATTN_WIDTH),
                  *([row(ATTN_NCHUNK * LANES)] * 3),
                  pl.BlockSpec(expand.shape, lambda b, i: (0, 0)),
                  row(3 * D_MODEL), row(D_MODEL),
                  full(wbs), full(wbp), full(wba), full(wout)],
        out_specs=row(D_MODEL),
        out_shape=jax.ShapeDtypeStruct((T, D_MODEL), F32),
        scratch_shapes=[pltpu.VMEM((tm + 2 * POOL_HALO, POOL_WIDTH), F32)],
        compiler_params=_cparams("parallel", "parallel"),
        name="merge",
    )(y_dirs, y_dirs, us5_tm, dsk, wglu, u_pool, u_pool, u_pool, pool_w, pool_scale,
      *outs, *stats, expand, gates, h2d, wbs, wbp, wba, wout)


def _mlp_kernel(h_ref, g_ref, wup_ref, wdn_ref, gf_ref, o_ref, *, final, chunk):
    h = h_ref[...]
    hn = _rms(h, g_ref[...]).astype(BF16)
    acc = h
    for c in range(D_FF // chunk):
        cols = slice(chunk * c, chunk * (c + 1))
        up = jnp.dot(hn, wup_ref[:, cols], preferred_element_type=F32)
        act = jnp.square(jnp.maximum(up, 0.0)).astype(BF16)
        acc = acc + jnp.dot(act, wdn_ref[cols, :], preferred_element_type=F32)
    if final:
        acc = _rms(acc, gf_ref[...])
    o_ref[...] = acc


def _mlp(h2d, g, wup, wdn, gf, layer, final, tm=1024, chunk=1024):
    T = h2d.shape[0]
    const = lambda a: _layer_spec(a, layer, pipeline_mode=pl.Buffered(1))
    return pl.pallas_call(
        functools.partial(_mlp_kernel, final=final, chunk=chunk),
        grid=(T // tm,),
        in_specs=[pl.BlockSpec((tm, D_MODEL), lambda m: (m, 0)),
                  const(g), const(wup), const(wdn),
                  pl.BlockSpec(gf.shape, lambda m: (0, 0), pipeline_mode=pl.Buffered(1))],
        out_specs=pl.BlockSpec((tm, D_MODEL), lambda m: (m, 0)),
        out_shape=jax.ShapeDtypeStruct((T, D_MODEL), F32),
        compiler_params=_cparams("parallel"),
        name="mlp",
    )(h2d, g, wup, wdn, gf)


def kernel(x, norm_mix, w_in, s5_lam_re, s5_lam_im, s5_log_dt, s5_b_re, s5_b_im, s5_c_re, s5_c_im, s5_d, s5_w_glu, pool_w, pool_scale, w_branch_s5, w_branch_pool, w_branch_attn, w_out, norm_mlp, w_up, w_down, norm_final):
    bsz, seq, _ = x.shape
    depth = w_in.shape[0]
    bf = lambda t: t.astype(BF16)
    rows = lambda t: t.astype(F32).reshape(depth, 1, -1)
    h = x.astype(F32).reshape(bsz * seq, D_MODEL)
    s5_w = jax.vmap(_s5_params)(s5_lam_re, s5_lam_im, s5_log_dt, s5_b_re, s5_b_im, s5_c_re, s5_c_im)
    biases = [_attn_bias(dil) for dil in DILATIONS]
    norm_mix, norm_mlp, s5_d, pool_scale = map(rows, (norm_mix, norm_mlp, s5_d, pool_scale))
    (w_in, s5_w_glu, pool_w, w_branch_s5, w_branch_pool, w_branch_attn, w_out, w_up, w_down) = map(
        bf, (w_in, s5_w_glu, pool_w, w_branch_s5, w_branch_pool, w_branch_attn, w_out, w_up, w_down))
    norm_final = norm_final.astype(F32).reshape(1, -1)
    for l in range(depth):
        us5_tm, u_pool, q, k, v, gates = _inproj(h, norm_mix, w_in, l, bsz, seq)
        y_dirs = _s5_scan(us5_tm, *s5_w, l, bsz, seq)
        outs, stats = zip(*[_attn_pattern(q[i], k[i], v[i], biases[i], dil, bsz, seq)
                            for i, dil in enumerate(DILATIONS)])
        h = _merge(y_dirs, us5_tm, s5_d, s5_w_glu, u_pool, pool_w, pool_scale, outs, stats, gates, h,
                   w_branch_s5, w_branch_pool, w_branch_attn, w_out, l, bsz, seq)
        h = _mlp(h, norm_mlp, w_up, w_down, norm_final, l, l == depth - 1)
    return h.reshape(bsz, seq, D_MODEL).astype(x.dtype)
```

```python
import functools
import math

import jax
import jax.numpy as jnp
import numpy as np
from jax import lax
from jax.experimental import pallas as pl
from jax.experimental.pallas import tpu as pltpu

F32 = jnp.float32
BF16 = jnp.bfloat16

D_MODEL = 1024
S5_WIDTH = 512
S5_GROUP_SIZE = 16
S5_STATE = 64
POOL_WIDTH = 512
POOL_WINDOWS = (2, 4, 8, 16)
POOL_GROUP_SIZE = 128
ATTN_HEADS = 16
HEAD_DIM = 64
ATTN_WIDTH = 1024
ATTN_PATTERNS = ((128, 1), (512, 4), (2048, 16))
D_FF = 4096
NORM_EPS = 1e-6
NEG_INF = -1e30

LANES = 128
SUBLANES = 8
S5_UNIT_GROUPS = LANES // S5_GROUP_SIZE
S5_UNITS = S5_WIDTH // LANES
S5_UNIT_STATE = S5_UNIT_GROUPS * S5_STATE
QBLK = 128
KWIN = 2 * QBLK
ATTN_CHUNK = 256
ATTN_NCHUNK = ATTN_WIDTH // ATTN_CHUNK
VMEM_LIMIT = 56 * 1024 * 1024


def _cparams(*sem):
    return pltpu.CompilerParams(dimension_semantics=sem, vmem_limit_bytes=VMEM_LIMIT)


def _rms(x, g):
    ms = jnp.mean(x * x, axis=-1, keepdims=True)
    return x * lax.rsqrt(ms + NORM_EPS) * g


def _layer_spec(a, layer, **kw):
    rest = (0,) * (a.ndim - 1)
    return pl.BlockSpec((None,) + a.shape[1:], lambda *_: (layer,) + rest, **kw)


def _sigmoid(x):
    return 0.5 + 0.5 * jnp.tanh(0.5 * x)


DILATIONS = tuple(dil for _, dil in ATTN_PATTERNS)
LOG2E = math.log2(math.e)
Q_SCALE = HEAD_DIM ** -0.5 * LOG2E


def _inproj_kernel(x_ref, g_ref, w_ref, us5_ref, upool_ref, *rest, tm):
    nd = len(DILATIONS)
    qkv_refs = [rest[nd * i:nd * (i + 1)] for i in range(3)]
    gate_ref = rest[3 * nd]
    lay_refs = dict(zip(DILATIONS[:-1], rest[3 * nd + 1:]))
    xn = _rms(x_ref[...], g_ref[...]).astype(BF16)
    proj = lambda c: jnp.dot(xn, w_ref[:, D_MODEL * c:D_MODEL * (c + 1)], preferred_element_type=F32)

    for c in range(3):
        gate_ref[:, D_MODEL * c:D_MODEL * (c + 1)] = _sigmoid(proj(4 + c)).astype(BF16)

    for i, refs in enumerate(qkv_refs):
        acc = proj(1 + i)
        if i == 0:
            acc = acc * Q_SCALE
        refs[0][0] = acc.astype(BF16)
        for j in range(ATTN_WIDTH // LANES):
            lay_refs[1][j] = acc[:, LANES * j:LANES * (j + 1)]
        for base, dil, ref in zip(DILATIONS[:-1], DILATIONS[1:], refs[1:]):
            ratio, rows = dil // base, tm // dil
            for c in range(base):
                for a in range(ratio):
                    r = c + base * a
                    for j in range(ATTN_WIDTH // LANES):
                        piece = lay_refs[base][j, pl.ds(c * (tm // base) + a, rows, stride=ratio), :]
                        ref[r, :, LANES * j:LANES * (j + 1)] = piece.astype(BF16)
                        if dil in lay_refs:
                            lay_refs[dil][j, r * rows:(r + 1) * rows, :] = piece

    acc = proj(0)
    us5_ref[...] = acc[:, :S5_WIDTH].astype(BF16)
    upool_ref[...] = acc[:, S5_WIDTH:]


def _inproj(h2d, g, w_bf16, layer, bsz, seq, tm=512):
    T = bsz * seq
    tps = seq // tm
    row = lambda m: (m, 0)
    const = lambda a: _layer_spec(a, layer, pipeline_mode=pl.Buffered(1))
    lay_specs = [pl.BlockSpec((None, dil, tm // dil, ATTN_WIDTH), lambda m: (m // tps, 0, m % tps, 0))
                 for dil in DILATIONS]
    lay_shapes = [jax.ShapeDtypeStruct((bsz, dil, seq // dil, ATTN_WIDTH), BF16) for dil in DILATIONS]
    nd = len(DILATIONS)
    outs = pl.pallas_call(
        functools.partial(_inproj_kernel, tm=tm),
        grid=(T // tm,),
        in_specs=[pl.BlockSpec((tm, D_MODEL), row), const(g), const(w_bf16)],
        out_specs=[
            pl.BlockSpec((tm, S5_WIDTH), lambda m: (m % tps, m // tps)),
            pl.BlockSpec((tm, POOL_WIDTH), row),
            *(lay_specs * 3),
            pl.BlockSpec((tm, 3 * D_MODEL), row),
        ],
        out_shape=[
            jax.ShapeDtypeStruct((seq, bsz * S5_WIDTH), BF16),
            jax.ShapeDtypeStruct((T, POOL_WIDTH), F32),
            *(lay_shapes * 3),
            jax.ShapeDtypeStruct((T, 3 * D_MODEL), BF16),
        ],
        scratch_shapes=[pltpu.VMEM((ATTN_WIDTH // LANES, tm, LANES), F32)] * (nd - 1),
        compiler_params=_cparams("parallel"),
        name="inproj",
    )(h2d, g, w_bf16)
    q, k, v = (outs[2 + nd * i:2 + nd * (i + 1)] for i in range(3))
    return outs[0], outs[1], q, k, v, outs[2 + 3 * nd]


def _s5_kernel(u_ref, bu_ref, cu_ref, cb_ref, a_ref, y_ref, st_ref, tmp_ref, xe_ref, xo_ref,
               p_ref, ypar_ref, ynat_ref, *sig_refs, tt, bsz):
    d = pl.program_id(0)
    i = pl.program_id(1)
    hs = S5_UNIT_STATE
    npair = tt // 2
    prows = npair * bsz
    nu = S5_UNITS

    @pl.when(i == 0)
    def _():
        st_ref[...] = jnp.zeros_like(st_ref)

    for b in range(bsz):
        for j in range(nu):
            lanes = slice(S5_WIDTH * b + LANES * j, S5_WIDTH * b + LANES * (j + 1))
            tmp_ref[b * nu + j] = u_ref[:, lanes].astype(F32)
            xe_ref[j, pl.ds(b, npair, stride=bsz), :] = tmp_ref[b * nu + j, pl.ds(0, npair, stride=2), :]
            xo_ref[j, pl.ds(b, npair, stride=bsz), :] = tmp_ref[b * nu + j, pl.ds(1, npair, stride=2), :]

    lead = pl.multiple_of((1 - d) * bsz, bsz)
    carry_rows = pl.ds(pl.multiple_of(d * prows, bsz), bsz)
    x2 = []
    for j in range(nu):
        x2.append(jnp.concatenate([xe_ref[j], xo_ref[j]], axis=1).astype(BF16))
        sig_refs[j][pl.ds(lead, prows), :] = jnp.dot(x2[j], bu_ref[j], preferred_element_type=F32)
        sig_refs[j][carry_rows, :] = st_ref[:, 2 * hs * j:2 * hs * (j + 1)]

    for j0 in range(0, nu, 2):
        units = (j0, j0 + 1)
        coef = [(jnp.broadcast_to(a_ref[0:1, hs * j:hs * (j + 1)], (bsz, hs)),
                 jnp.broadcast_to(a_ref[1:2, hs * j:hs * (j + 1)], (bsz, hs))) for j in units]
        init = tuple((st_ref[:, 2 * hs * j:2 * hs * j + hs], st_ref[:, 2 * hs * j + hs:2 * hs * (j + 1)])
                     for j in units)

        def step(s, carry, units=units, coef=coef):
            m = jnp.where(d == 0, s, npair - 1 - s)
            rows = pl.ds(pl.multiple_of(m * bsz + lead, bsz), bsz)
            out = []
            for j, (ar, ai), (sr, si) in zip(units, coef, carry):
                sig = sig_refs[j]
                nr = ar * sr - ai * si + sig[rows, :hs]
                ni = ar * si + ai * sr + sig[rows, hs:]
                sig[rows, :hs] = nr
                sig[rows, hs:] = ni
                out.append((nr, ni))
            return tuple(out)

        final = lax.fori_loop(0, npair, step, init, unroll=8)
        for j, (sr, si) in zip(units, final):
            st_ref[:, 2 * hs * j:2 * hs * j + hs] = sr
            st_ref[:, 2 * hs * j + hs:2 * hs * (j + 1)] = si

    for j in range(nu):
        p_ref[j] = jnp.dot(sig_refs[j][...].astype(BF16), cu_ref[j], preferred_element_type=F32)
        ypar_ref[0, j] = p_ref[j, pl.ds(lead, prows), :LANES]
        ypar_ref[1, j] = (p_ref[j, pl.ds(bsz - lead, prows), LANES:]
                          + jnp.dot(x2[j], cb_ref[j], preferred_element_type=F32))

    for b in range(bsz):
        for j in range(nu):
            lanes = slice(S5_WIDTH * b + LANES * j, S5_WIDTH * b + LANES * (j + 1))
            k = b * nu + j
            ynat_ref[k, pl.ds(1 - d, npair, stride=2), :] = ypar_ref[0, j, pl.ds(b, npair, stride=bsz), :]
            ynat_ref[k, pl.ds(d, npair, stride=2), :] = ypar_ref[1, j, pl.ds(b, npair, stride=bsz), :]
            y_ref[:, lanes] = ynat_ref[k].astype(BF16)


def _s5_scan(u_tm, bu, cu, cb, a2, layer, bsz, seq, tt=128):
    assert bsz == SUBLANES and tt % 2 == 0
    n_t = seq // tt
    tile = lambda d, i: i + d * (n_t - 1 - 2 * i)
    uw = 2 * S5_UNIT_STATE
    prows = tt // 2 * bsz
    per_dir = lambda a: pl.BlockSpec((None, None) + a.shape[2:],
                                     lambda d, i: (layer, d) + (0,) * (a.ndim - 2))
    return pl.pallas_call(
        functools.partial(_s5_kernel, tt=tt, bsz=bsz),
        grid=(2, n_t),
        in_specs=[pl.BlockSpec((tt, bsz * S5_WIDTH), lambda d, i: (tile(d, i), 0)),
                  per_dir(bu), per_dir(cu), per_dir(cb), per_dir(a2)],
        out_specs=pl.BlockSpec((None, tt, bsz * S5_WIDTH), lambda d, i: (d, tile(d, i), 0)),
        out_shape=jax.ShapeDtypeStruct((2, seq, bsz * S5_WIDTH), BF16),
        scratch_shapes=[pltpu.VMEM((bsz, S5_UNITS * uw), F32),
                        pltpu.VMEM((bsz * S5_UNITS, tt, LANES), F32),
                        pltpu.VMEM((S5_UNITS, prows, LANES), F32),
                        pltpu.VMEM((S5_UNITS, prows, LANES), F32),
                        pltpu.VMEM((S5_UNITS, prows + bsz, 2 * LANES), F32),
                        pltpu.VMEM((2, S5_UNITS, prows, LANES), F32),
                        pltpu.VMEM((bsz * S5_UNITS, tt, LANES), F32)]
        + [pltpu.VMEM((prows + bsz, uw), F32)] * S5_UNITS,
        compiler_params=_cparams("arbitrary", "arbitrary"),
        name="s5_scan",
    )(u_tm, bu, cu, cb, a2)


def _s5_params(lam_re, lam_im, log_dt, b_re, b_im, c_re, c_im):
    f = lambda t: t.astype(F32)
    lam_re, lam_im, b_re, b_im, c_re, c_im = map(f, (lam_re, lam_im, b_re, b_im, c_re, c_im))
    dt = jnp.exp(f(log_dt))[:, :, None]
    mag = jnp.exp(lam_re * dt)
    a_re = mag * jnp.cos(lam_im * dt)
    a_im = mag * jnp.sin(lam_im * dt)
    den = lam_re * lam_re + lam_im * lam_im
    f_re = ((a_re - 1.0) * lam_re + a_im * lam_im) / den
    f_im = (a_im * lam_re - (a_re - 1.0) * lam_im) / den
    bb_re = f_re[..., None] * b_re - f_im[..., None] * b_im
    bb_im = f_re[..., None] * b_im + f_im[..., None] * b_re
    ug, nu = S5_UNIT_GROUPS, S5_UNITS

    def block_diag(t, rows_per_group):
        r, cols = t.shape[2], t.shape[3]
        same = (np.arange(ug * r)[:, None] // rows_per_group
                == np.arange(cols)[None, :] // (cols // ug)).astype(np.float32)
        return jnp.broadcast_to(t[:, :, None], (2, nu, ug, r, cols)).reshape(2, nu, ug * r, cols) * same

    def pack_b(t):
        t = t.reshape(2, nu, ug, S5_STATE, S5_GROUP_SIZE).transpose(0, 1, 4, 2, 3)
        return block_diag(t.reshape(2, nu, S5_GROUP_SIZE, S5_UNIT_STATE), S5_GROUP_SIZE)

    def pack_c(t):
        t = t.reshape(2, nu, ug, S5_GROUP_SIZE, S5_STATE).transpose(0, 1, 4, 2, 3)
        return block_diag(t.reshape(2, nu, S5_STATE, LANES), S5_STATE)

    def pack_pp(t):
        t = t.reshape(2, nu, ug, S5_GROUP_SIZE, S5_GROUP_SIZE).transpose(0, 1, 4, 2, 3)
        return block_diag(t.reshape(2, nu, S5_GROUP_SIZE, LANES), S5_GROUP_SIZE)

    ab_re = a_re[..., None] * bb_re - a_im[..., None] * bb_im
    ab_im = a_re[..., None] * bb_im + a_im[..., None] * bb_re
    b_plain = jnp.concatenate([pack_b(bb_re), pack_b(bb_im)], axis=-1)
    b_first = jnp.concatenate([pack_b(ab_re), pack_b(ab_im)], axis=-1)
    bu = jnp.stack([jnp.concatenate([b_first[0], b_plain[0]], axis=-2),
                    jnp.concatenate([b_plain[1], b_first[1]], axis=-2)]).astype(BF16)

    ca_re = c_re * a_re[:, :, None, :] - c_im * a_im[:, :, None, :]
    ca_im = c_re * a_im[:, :, None, :] + c_im * a_re[:, :, None, :]
    c_plain = jnp.concatenate([pack_c(c_re), -pack_c(c_im)], axis=-2)
    c_a = jnp.concatenate([pack_c(ca_re), -pack_c(ca_im)], axis=-2)
    cu = jnp.concatenate([c_plain, c_a], axis=-1).astype(BF16)

    cb_pq = (jnp.einsum('dgpn,dgnq->dgpq', c_re, bb_re) - jnp.einsum('dgpn,dgnq->dgpq', c_im, bb_im))
    cb_mat = pack_pp(cb_pq)
    zero = jnp.zeros_like(cb_mat[0])
    cb = jnp.stack([jnp.concatenate([cb_mat[0], zero], axis=-2),
                    jnp.concatenate([zero, cb_mat[1]], axis=-2)]).astype(BF16)

    a2_re = a_re * a_re - a_im * a_im
    a2_im = 2.0 * a_re * a_im
    a2 = jnp.stack([a2_re.reshape(2, -1), a2_im.reshape(2, -1)], axis=1)
    return bu, cu, cb, a2


POOL_HALO = SUBLANES
POOL_ROWS = 256


def _pool_mix(xs_ref, w_ref, s_ref, t0, seq, tm):
    n = POOL_ROWS + 2 * POOL_HALO
    out = []
    for c in range(tm // POOL_ROWS):
        t = t0 + POOL_ROWS * c + lax.broadcasted_iota(jnp.int32, (POOL_ROWS, 1), 0)
        groups = []
        for g, win in enumerate(POOL_WINDOWS):
            lanes = slice(POOL_GROUP_SIZE * g, POOL_GROUP_SIZE * (g + 1))
            x = xs_ref[POOL_ROWS * c:POOL_ROWS * c + n, lanes]
            s = x
            k = 1
            while k < win:
                s = s + pltpu.roll(s, k, axis=0)
                k *= 2
            left = win // 2
            right = win - 1 - left
            if right:
                s = pltpu.roll(s, n - right, axis=0)
            hi = jnp.minimum(t + right + 1, seq)
            lo = jnp.maximum(t - left, 0)
            mean = s[POOL_HALO:POOL_HALO + POOL_ROWS] / (hi - lo).astype(F32)
            pooled = mean - x[POOL_HALO:POOL_HALO + POOL_ROWS]
            mixed = jnp.dot(pooled.astype(BF16), w_ref[g], preferred_element_type=F32)
            groups.append((mixed * s_ref[:, lanes]).astype(BF16))
        out.append(jnp.concatenate(groups, axis=1))
    return jnp.concatenate(out, axis=0)


def _stat_lane(head):
    return head + HEAD_DIM * (1 - head % 2)


STAT_MAX_SHIFT = 8


def _out_pitch(dil):
    return dil + SUBLANES if dil % (2 * SUBLANES) == 0 else dil


def _attn_kernel(q_ref, k_ref, v_ref, bias_ref, o_ref, stat_ref, *scr, dil, m_rows, pairs, unroll):
    hc = pl.program_id(1)
    nqb = m_rows // QBLK
    pitch = _out_pitch(dil)
    stat = stat_ref if dil == 1 else scr[1]
    stat[...] = jnp.ones_like(stat)

    low = lax.broadcasted_iota(jnp.int32, (QBLK, LANES), 1) < HEAD_DIM

    def qblock(it, carry):
        r = it // nqb
        m0 = pl.multiple_of((it % nqb) * QBLK, QBLK)
        ks = pl.multiple_of(jnp.clip(m0 - QBLK // 2, 0, m_rows - KWIN), QBLK // 2)
        var = (m0 - ks) // (QBLK // 2)
        qrows = pl.ds(m0, QBLK)
        krows = pl.ds(ks, KWIN)
        srows = pl.ds(pl.multiple_of(r * m_rows + m0, QBLK), QBLK)
        for j in range(pairs):
            lanes = slice(LANES * j, LANES * (j + 1))
            qp = q_ref[r, qrows, lanes]
            zero = jnp.zeros_like(qp)
            q2 = jnp.concatenate([jnp.where(low, qp, zero), jnp.where(low, zero, qp)], axis=0)
            s2 = lax.dot_general(q2, k_ref[r, krows, lanes], (((1,), (1,)), ((), ())),
                                 preferred_element_type=F32)
            vp = v_ref[r, krows, lanes]
            halves = []
            for e in range(2):
                s = s2[QBLK * e:QBLK * (e + 1)] + bias_ref[var, 2 * (hc * pairs + j) + e]
                mx = jnp.max(s, axis=-1, keepdims=True)
                p = jnp.exp2(s - mx)
                halves.append(jnp.dot(p.astype(BF16), vp, preferred_element_type=F32))
                sl = _stat_lane(2 * j + e)
                stat[srows, sl:sl + 1] = jnp.sum(p, axis=-1, keepdims=True)
                stat[srows, sl + STAT_MAX_SHIFT:sl + STAT_MAX_SHIFT + 1] = mx
            o_pair = jnp.where(low, halves[0], halves[1])
            if dil == 1:
                o_ref[qrows, lanes] = o_pair.astype(BF16)
            else:
                scr[0][j, pl.ds(r + pitch * m0, QBLK, stride=pitch), :] = o_pair
        return carry

    lax.fori_loop(0, dil * nqb, qblock, 0, unroll=unroll)
    if dil > 1:
        for j in range(pairs):
            lanes = slice(LANES * j, LANES * (j + 1))
            if pitch == dil:
                o_ref[:, lanes] = scr[0][j].astype(BF16)
            else:
                for m in range(m_rows):
                    o_ref[dil * m:dil * (m + 1), lanes] = scr[0][j, pitch * m:pitch * m + dil, :].astype(BF16)
        for r in range(dil):
            rows = slice(r * m_rows, (r + 1) * m_rows)
            stat_ref[pl.ds(r, m_rows, stride=dil), :] = stat[rows, :]


def _attn_bias(dil):
    slopes = np.exp2(-8.0 * np.arange(1, ATTN_HEADS + 1, dtype=np.float32) / ATTN_HEADS)
    a = np.arange(QBLK)[:, None]
    c = np.arange(KWIN)[None, :]
    off = (np.arange(3) * (QBLK // 2))[:, None, None]
    rel = c[None] - off - a[None]
    dist = (dil * np.abs(rel)).astype(np.float32)
    bias = np.float32(-LOG2E) * slopes[None, :, None, None] * dist[:, None]
    bias = np.where((np.abs(rel) <= QBLK // 2)[:, None], bias, np.float32(NEG_INF))
    return jnp.asarray(bias.astype(np.float32))


def _attn_pattern(q, k, v, bias, dil, bsz, seq, hw=ATTN_CHUNK, unroll=32):
    m_rows = seq // dil
    nhc = ATTN_WIDTH // hw
    pairs = hw // LANES
    blk = pl.BlockSpec((None, dil, m_rows, hw), lambda b, h: (b, 0, 0, h))
    stat = pl.BlockSpec((seq, LANES), lambda b, h: (b, h))
    regroup = [pltpu.VMEM((pairs, m_rows * _out_pitch(dil), LANES), F32), pltpu.VMEM((seq, LANES), F32)]
    return pl.pallas_call(
        functools.partial(_attn_kernel, dil=dil, m_rows=m_rows, pairs=pairs, unroll=unroll),
        grid=(bsz, nhc),
        in_specs=[blk, blk, blk,
                  pl.BlockSpec(bias.shape, lambda b, h: (0, 0, 0, 0), pipeline_mode=pl.Buffered(1))],
        out_specs=[pl.BlockSpec((seq, hw), lambda b, h: (b, h)), stat],
        out_shape=[jax.ShapeDtypeStruct((bsz * seq, ATTN_WIDTH), BF16),
                   jax.ShapeDtypeStruct((bsz * seq, nhc * LANES), F32)],
        scratch_shapes=regroup if dil > 1 else [],
        compiler_params=_cparams("parallel", "parallel"),
        name=f"attn_d{dil}",
    )(q, k, v, bias)


def _merge_kernel(yf_ref, yb_ref, us5_ref, dsk_ref, wglu_ref,
                  up_ref, uprev_ref, unext_ref, pw_ref, ps_ref,
                  o1_ref, o2_ref, o3_ref, s1_ref, s2_ref, s3_ref, exp_ref,
                  gate_ref, h_ref, wbs_ref, wbp_ref, wba_ref, wout_ref, out_ref, xs_ref, *, seq):
    dot = functools.partial(jnp.dot, preferred_element_type=F32)
    tm = h_ref.shape[0]
    i = pl.program_id(1)
    y = (yf_ref[...].astype(F32) + yb_ref[...].astype(F32)
         + dsk_ref[...] * us5_ref[...].astype(F32))
    y = jax.nn.gelu(y)
    y = y * _sigmoid(dot(y.astype(BF16), wglu_ref[...]))
    br_s5 = dot(y.astype(BF16), wbs_ref[...])

    xs_ref[0:POOL_HALO, :] = jnp.where(i > 0, uprev_ref[...], 0.0)
    xs_ref[POOL_HALO:POOL_HALO + tm, :] = up_ref[...]
    xs_ref[POOL_HALO + tm:, :] = jnp.where(i < pl.num_programs(1) - 1, unext_ref[...], 0.0)
    br_pool = dot(_pool_mix(xs_ref, pw_ref, ps_ref, i * tm, seq, tm), wbp_ref[...])

    lane = lax.broadcasted_iota(jnp.int32, (h_ref.shape[0], LANES), 1)
    is_stat = functools.reduce(jnp.logical_or,
                               [lane == _stat_lane(hd) for hd in range(ATTN_CHUNK // HEAD_DIM)])
    att = []
    for c in range(ATTN_NCHUNK):
        st, ch = slice(LANES * c, LANES * (c + 1)), slice(ATTN_CHUNK * c, ATTN_CHUNK * (c + 1))
        d1, d2, d3 = s1_ref[:, st], s2_ref[:, st], s3_ref[:, st]
        m1, m2, m3 = (pltpu.roll(s, LANES - STAT_MAX_SHIFT, axis=1) for s in (d1, d2, d3))
        mx = jnp.maximum(jnp.maximum(m1, m2), m3)
        w1, w2, w3 = jnp.exp2(m1 - mx), jnp.exp2(m2 - mx), jnp.exp2(m3 - mx)
        inv = 1.0 / (w1 * d1 + w2 * d2 + w3 * d3)
        expand = lambda w: dot(jnp.where(is_stat, w * inv, 0.0).astype(BF16), exp_ref[...]).astype(BF16)
        att.append(expand(w1) * o1_ref[:, ch] + expand(w2) * o2_ref[:, ch] + expand(w3) * o3_ref[:, ch])
    br_attn = dot(jnp.concatenate(att, axis=1), wba_ref[...])

    merged = (gate_ref[:, 0:D_MODEL] * br_s5.astype(BF16)
              + gate_ref[:, D_MODEL:2 * D_MODEL] * br_pool.astype(BF16)
              + gate_ref[:, 2 * D_MODEL:3 * D_MODEL] * br_attn.astype(BF16))
    out_ref[...] = h_ref[...] + dot(merged, wout_ref[...])


def _merge(y_dirs, us5_tm, dsk, wglu, u_pool, pool_w, pool_scale, outs, stats, gates, h2d,
           wbs, wbp, wba, wout, layer, bsz, seq, tm=512):
    T = bsz * seq
    tps = seq // tm
    hpt = tm // POOL_HALO
    row = lambda w: pl.BlockSpec((tm, w), lambda b, i: (b * tps + i, 0))
    full = lambda a: _layer_spec(a, layer)
    ydir = lambda d: pl.BlockSpec((None, tm, S5_WIDTH), lambda b, i: (d, i, b))
    halo_prev = pl.BlockSpec((POOL_HALO, POOL_WIDTH),
                             lambda b, i: (jnp.maximum((b * tps + i) * hpt - 1, 0), 0))
    halo_next = pl.BlockSpec((POOL_HALO, POOL_WIDTH),
                             lambda b, i: (jnp.minimum((b * tps + i + 1) * hpt, T // POOL_HALO - 1), 0))
    local_head = np.arange(ATTN_CHUNK)[None, :] // HEAD_DIM
    expand = jnp.asarray(np.arange(LANES)[:, None] == _stat_lane(local_head), BF16)
    return pl.pallas_call(
        functools.partial(_merge_kernel, seq=seq),
        grid=(bsz, tps),
        in_specs=[ydir(0), ydir(1),
                  pl.BlockSpec((tm, S5_WIDTH), lambda b, i: (i, b)),
                  full(dsk), full(wglu),
                  row(POOL_WIDTH), halo_prev, halo_next, full(pool_w), full(pool_scale),
                  row(ATTN_WIDTH), row(ATTN_WIDTH), row(ATTN_WIDTH),
                  *([row(ATTN_NCHUNK * LANES)] * 3),
                  pl.BlockSpec(expand.shape, lambda b, i: (0, 0)),
                  row(3 * D_MODEL), row(D_MODEL),
                  full(wbs), full(wbp), full(wba), full(wout)],
        out_specs=row(D_MODEL),
        out_shape=jax.ShapeDtypeStruct((T, D_MODEL), F32),
        scratch_shapes=[pltpu.VMEM((tm + 2 * POOL_HALO, POOL_WIDTH), F32)],
        compiler_params=_cparams("parallel", "parallel"),
        name="merge",
    )(y_dirs, y_dirs, us5_tm, dsk, wglu, u_pool, u_pool, u_pool, pool_w, pool_scale,
      *outs, *stats, expand, gates, h2d, wbs, wbp, wba, wout)


def _mlp_kernel(h_ref, g_ref, wup_ref, wdn_ref, gf_ref, o_ref, *, final, chunk):
    h = h_ref[...]
    hn = _rms(h, g_ref[...]).astype(BF16)
    acc = h
    for c in range(D_FF // chunk):
        cols = slice(chunk * c, chunk * (c + 1))
        up = jnp.dot(hn, wup_ref[:, cols], preferred_element_type=F32)
        act = jnp.square(jnp.maximum(up, 0.0)).astype(BF16)
        acc = acc + jnp.dot(act, wdn_ref[cols, :], preferred_element_type=F32)
    if final:
        acc = _rms(acc, gf_ref[...])
    o_ref[...] = acc


def _mlp(h2d, g, wup, wdn, gf, layer, final, tm=1024, chunk=1024):
    T = h2d.shape[0]
    const = lambda a: _layer_spec(a, layer, pipeline_mode=pl.Buffered(1))
    return pl.pallas_call(
        functools.partial(_mlp_kernel, final=final, chunk=chunk),
        grid=(T // tm,),
        in_specs=[pl.BlockSpec((tm, D_MODEL), lambda m: (m, 0)),
                  const(g), const(wup), const(wdn),
                  pl.BlockSpec(gf.shape, lambda m: (0, 0), pipeline_mode=pl.Buffered(1))],
        out_specs=pl.BlockSpec((tm, D_MODEL), lambda m: (m, 0)),
        out_shape=jax.ShapeDtypeStruct((T, D_MODEL), F32),
        compiler_params=_cparams("parallel"),
        name="mlp",
    )(h2d, g, wup, wdn, gf)


def kernel(x, norm_mix, w_in, s5_lam_re, s5_lam_im, s5_log_dt, s5_b_re, s5_b_im, s5_c_re, s5_c_im, s5_d, s5_w_glu, pool_w, pool_scale, w_branch_s5, w_branch_pool, w_branch_attn, w_out, norm_mlp, w_up, w_down, norm_final):
    bsz, seq, _ = x.shape
    depth = w_in.shape[0]
    bf = lambda t: t.astype(BF16)
    rows = lambda t: t.astype(F32).reshape(depth, 1, -1)
    h = x.astype(F32).reshape(bsz * seq, D_MODEL)
    s5_w = jax.vmap(_s5_params)(s5_lam_re, s5_lam_im, s5_log_dt, s5_b_re, s5_b_im, s5_c_re, s5_c_im)
    biases = [_attn_bias(dil) for dil in DILATIONS]
    norm_mix, norm_mlp, s5_d, pool_scale = map(rows, (norm_mix, norm_mlp, s5_d, pool_scale))
    (w_in, s5_w_glu, pool_w, w_branch_s5, w_branch_pool, w_branch_attn, w_out, w_up, w_down) = map(
        bf, (w_in, s5_w_glu, pool_w, w_branch_s5, w_branch_pool, w_branch_attn, w_out, w_up, w_down))
    norm_final = norm_final.astype(F32).reshape(1, -1)
    for l in range(depth):
        us5_tm, u_pool, q, k, v, gates = _inproj(h, norm_mix, w_in, l, bsz, seq)
        y_dirs = _s5_scan(us5_tm, *s5_w, l, bsz, seq)
        outs, stats = zip(*[_attn_pattern(q[i], k[i], v[i], biases[i], dil, bsz, seq)
                            for i, dil in enumerate(DILATIONS)])
        h = _merge(y_dirs, us5_tm, s5_d, s5_w_glu, u_pool, pool_w, pool_scale, outs, stats, gates, h,
                   w_branch_s5, w_branch_pool, w_branch_attn, w_out, l, bsz, seq)
        h = _mlp(h, norm_mlp, w_up, w_down, norm_final, l, l == depth - 1)
    return h.reshape(bsz, seq, D_MODEL).astype(x.dtype)
```
